```python
import jax, jax.numpy as jnp
from jax import lax
import numpy as np

D_MODEL = 4096
BATCH = 2
SEQ = 4096
DEPTH = 2
DEC_BATCH = 16
DEC_SEQ = 32
PAST_LEN = 1024

CHUNK = 64
N_MEM = 256
MIX_WIDTH = D_MODEL
MEM_HEADS = 4
MEM_WIDTH = MIX_WIDTH // 4
MEM_HD = MEM_WIDTH // MEM_HEADS
TOK_WIDTH = MIX_WIDTH - MEM_WIDTH
SGU_CHUNK = 128
SGU_GROUPS = 8
SGU_GW = TOK_WIDTH // SGU_GROUPS
B_HD = 128
B_HEADS = TOK_WIDTH // B_HD
B_BAND_CHUNKS = 8
B_REACH = B_BAND_CHUNKS * CHUNK
REL_CLIP = 128
PEER_HEADS = 8
PEER_NKEYS = 128
PEER_EXPERTS = PEER_NKEYS * PEER_NKEYS
PEER_DKEY = 256
PEER_TOPK = 16
PEER_BLOCK = 64
N_A = (DEPTH + 1) // 2
N_B = DEPTH // 2
ALPHA = (2 * DEPTH) ** 0.25
BETA = (8 * DEPTH) ** -0.25
LN_EPS = 1e-5

kernel_name = 'hybrid_streaming_sgu_bandattn_peer_step'


def layer_norm(x, g, b):
    xf = x.astype(jnp.float32)
    mu = jnp.mean(xf, axis=-1, keepdims=True)
    var = jnp.mean(jnp.square(xf - mu), axis=-1, keepdims=True)
    return ((xf - mu) * lax.rsqrt(var + LN_EPS)).astype(x.dtype) * g + b


def mem_kv(mem, w):
    B = mem.shape[0]
    k, v = jnp.split(mem @ w, 2, axis=-1)
    return (k.reshape(B, N_MEM, MEM_HEADS, MEM_HD), v.reshape(B, N_MEM, MEM_HEADS, MEM_HD))


def mem_attend(q, mk, mv):
    s = jnp.einsum('bthd,bmhd->bhtm', q, mk).astype(jnp.float32) * (MEM_HD ** -0.5)
    p = jax.nn.softmax(s, axis=-1).astype(q.dtype)
    return jnp.einsum('bhtm,bmhd->bthd', p, mv)


def sgu_branch(x, w_in, ln_g, ln_b, w_s, b_s):
    B, T, _ = x.shape
    h = x @ w_in
    uv = jax.nn.gelu(h[..., :2 * TOK_WIDTH])
    qm = h[..., 2 * TOK_WIDTH:]
    u, v = jnp.split(uv, 2, axis=-1)
    v = layer_norm(v, ln_g, ln_b)
    n = min(T, SGU_CHUNK)
    nc = T // n
    mask = jnp.tril(jnp.ones((n, n), dtype=bool))
    w = jnp.where(mask[None], w_s[:, :n, :n], jnp.zeros_like(w_s[:, :n, :n]))
    vc = v.reshape(B, nc, n, SGU_GROUPS, SGU_GW)
    s = jnp.einsum('gts,bcsgd->bctgd', w, vc) + b_s[:, :n].T[None, None, :, :, None]
    tok = u * s.reshape(B, T, TOK_WIDTH)
    return tok, qm, v


def band_qkv(x, w_in):
    B, T, _ = x.shape
    h = x @ w_in
    q = h[..., :TOK_WIDTH].reshape(B, T, B_HEADS, B_HD)
    k = h[..., TOK_WIDTH:2 * TOK_WIDTH].reshape(B, T, B_HEADS, B_HD)
    v = h[..., 2 * TOK_WIDTH:3 * TOK_WIDTH].reshape(B, T, B_HEADS, B_HD)
    qm = h[..., 3 * TOK_WIDTH:]
    return q, k, v, qm


def band_attend(q, k, v, q_pos, k_pos, rel_bias):
    s = jnp.einsum('bqhd,bkhd->bhqk', q, k).astype(jnp.float32) * (B_HD ** -0.5)
    rel = jnp.clip(q_pos[:, None] - k_pos[None, :], -REL_CLIP, REL_CLIP) + REL_CLIP
    s = s + rel_bias[:, rel].astype(jnp.float32)[None]
    s = jnp.where((k_pos >= 0)[None, None, None, :], s, -1e30)
    p = jax.nn.softmax(s, axis=-1).astype(q.dtype)
    return jnp.einsum('bhqk,bkhd->bqhd', p, v)


def chunk_band_prompt(q, k, v, rel_bias):
    B, S, H, hd = q.shape
    nc = S // CHUNK
    pad = ((0, 0), (B_REACH, 0), (0, 0), (0, 0))
    kp = jnp.pad(k, pad)
    vp = jnp.pad(v, pad)
    qc = jnp.moveaxis(q.reshape(B, nc, CHUNK, H, hd), 1, 0)

    def one_chunk(args):
        c, qi = args
        start = c * CHUNK
        ks = lax.dynamic_slice_in_dim(kp, start, B_REACH + CHUNK, axis=1)
        vs = lax.dynamic_slice_in_dim(vp, start, B_REACH + CHUNK, axis=1)
        q_pos = start + jnp.arange(CHUNK)
        k_pos = start - B_REACH + jnp.arange(B_REACH + CHUNK)
        return band_attend(qi, ks, vs, q_pos, k_pos, rel_bias)

    out = lax.map(one_chunk, (jnp.arange(nc), qc))
    return jnp.moveaxis(out, 0, 1).reshape(B, S, H * hd)


def chunk_band_sample(q, k, v, cache_k, cache_v, rel_bias):
    B, T, H, hd = q.shape
    L = cache_k.shape[1]
    k_all = jnp.concatenate([cache_k, k], axis=1)
    v_all = jnp.concatenate([cache_v, v], axis=1)
    q_pos = PAST_LEN + jnp.arange(T)
    k_pos = PAST_LEN - L + jnp.arange(L + T)
    return band_attend(q, k_all, v_all, q_pos, k_pos, rel_bias).reshape(B, T, H * hd)


def peer_ffn(x, wq, keys, u_tab, v_tab):
    B, T, D = x.shape
    n = B * T
    npad = (-n) % PEER_BLOCK
    xt = jnp.pad(x.reshape(n, D), ((0, npad), (0, 0)))
    m = n + npad
    q = (xt @ wq).reshape(m, PEER_HEADS, 2, PEER_DKEY // 2)
    s = jnp.einsum('thpd,hpnd->thpn', q, keys).astype(jnp.float32)
    top_s, top_i = lax.top_k(s, PEER_TOPK)
    cand_s = (top_s[:, :, 0, :, None] + top_s[:, :, 1, None, :]).reshape(m, PEER_HEADS, PEER_TOPK * PEER_TOPK)
    cand_i = (top_i[:, :, 0, :, None] * PEER_NKEYS + top_i[:, :, 1, None, :]).reshape(m, PEER_HEADS, PEER_TOPK * PEER_TOPK)
    best_s, pos = lax.top_k(cand_s, PEER_TOPK)
    idx = jnp.take_along_axis(cand_i, pos, axis=-1)
    g = jax.nn.softmax(best_s, axis=-1)
    hk = PEER_HEADS * PEER_TOPK
    nb = m // PEER_BLOCK

    def one_block(args):
        xb, ib, gb = args
        act = jnp.einsum('td,ted->te', xb, u_tab[ib])
        a = gb.astype(xb.dtype) * jax.nn.gelu(act)
        return jnp.einsum('te,ted->td', a, v_tab[ib])

    y = lax.map(one_block, (xt.reshape(nb, PEER_BLOCK, D), idx.reshape(nb, PEER_BLOCK, hk), g.reshape(nb, PEER_BLOCK, hk)))
    return y.reshape(m, D)[:n].reshape(B, T, D)


def residual_block(x, tok, qm, mk, mv, w_out, ln1g, ln1b, ln2g, ln2b, wq, keys, u_tab, v_tab):
    B, T, _ = x.shape
    mo = mem_attend(qm.reshape(B, T, MEM_HEADS, MEM_HD), mk, mv).reshape(B, T, MEM_WIDTH)
    x = layer_norm(ALPHA * x + jnp.concatenate([tok, mo], axis=-1) @ w_out, ln1g, ln1b)
    return layer_norm(ALPHA * x + peer_ffn(x, wq, keys, u_tab, v_tab), ln2g, ln2b)


def setup_inputs(seed: int = 0) -> dict:
    key = jax.random.key(seed)
    ks = jax.random.split(key, 32)
    nrm = jax.random.normal
    D = D_MODEL
    cache_len = min(B_REACH, PAST_LEN)
    col_a = jnp.concatenate([jnp.full((2 * TOK_WIDTH,), BETA), jnp.ones((MEM_WIDTH,))])
    col_b = jnp.concatenate([jnp.ones((2 * TOK_WIDTH,)), jnp.full((TOK_WIDTH,), BETA), jnp.ones((MEM_WIDTH,))])
    col_m = jnp.concatenate([jnp.ones((MEM_WIDTH,)), jnp.full((MEM_WIDTH,), BETA)])
    return {
        'x_prompt': nrm(ks[0], (BATCH, SEQ, D)),
        'x_sample': nrm(ks[1], (DEC_BATCH, DEC_SEQ, D)),
        'cache_b_k': nrm(ks[2], (N_B, DEC_BATCH, cache_len, B_HEADS, B_HD)),
        'cache_b_v': nrm(ks[3], (N_B, DEC_BATCH, cache_len, B_HEADS, B_HD)) * BETA,
        'cache_mem_k': nrm(ks[4], (DEPTH, DEC_BATCH, N_MEM, MEM_HEADS, MEM_HD)),
        'cache_mem_v': nrm(ks[5], (DEPTH, DEC_BATCH, N_MEM, MEM_HEADS, MEM_HD)) * BETA,
        'mem_prompt': nrm(ks[6], (BATCH, N_MEM, D)),
        'w_in_a': nrm(ks[7], (N_A, D, 2 * TOK_WIDTH + MEM_WIDTH)) * (D ** -0.5) * col_a,
        'sgu_ln_g': 1.0 + 0.05 * nrm(ks[8], (N_A, TOK_WIDTH)),
        'sgu_ln_b': 0.05 * nrm(ks[9], (N_A, TOK_WIDTH)),
        'sgu_w': nrm(ks[10], (N_A, SGU_GROUPS, SGU_CHUNK, SGU_CHUNK)) * (SGU_CHUNK ** -0.5),
        'sgu_b': 1.0 + 0.1 * nrm(ks[11], (N_A, SGU_GROUPS, SGU_CHUNK)),
        'w_in_b': nrm(ks[12], (N_B, D, 3 * TOK_WIDTH + MEM_WIDTH)) * (D ** -0.5) * col_b,
        'rel_bias': 0.5 * nrm(ks[13], (N_B, B_HEADS, 2 * REL_CLIP + 1)),
        'w_mem_kv': nrm(ks[14], (DEPTH, D, 2 * MEM_WIDTH)) * (D ** -0.5) * col_m,
        'w_out': nrm(ks[15], (DEPTH, MIX_WIDTH, D)) * (MIX_WIDTH ** -0.5) * BETA,
        'ln1_g': 1.0 + 0.05 * nrm(ks[16], (DEPTH, D)),
        'ln1_b': 0.05 * nrm(ks[17], (DEPTH, D)),
        'ln2_g': 1.0 + 0.05 * nrm(ks[18], (DEPTH, D)),
        'ln2_b': 0.05 * nrm(ks[19], (DEPTH, D)),
        'peer_wq': nrm(ks[20], (DEPTH, D, PEER_HEADS * PEER_DKEY)) * (D ** -0.5),
        'peer_keys': nrm(ks[21], (DEPTH, PEER_HEADS, 2, PEER_NKEYS, PEER_DKEY // 2)) * ((PEER_DKEY // 2) ** -0.5),
        'peer_u': nrm(ks[22], (DEPTH, PEER_EXPERTS, D)) * (D ** -0.5) * BETA,
        'peer_v': nrm(ks[23], (DEPTH, PEER_EXPERTS, D)) * BETA,
    }


def reference(x_prompt, x_sample, cache_b_k, cache_b_v, cache_mem_k, cache_mem_v, mem_prompt,
              w_in_a, sgu_ln_g, sgu_ln_b, sgu_w, sgu_b, w_in_b, rel_bias, w_mem_kv, w_out,
              ln1_g, ln1_b, ln2_g, ln2_b, peer_wq, peer_keys, peer_u, peer_v):
    xp, xs = x_prompt, x_sample
    S = xp.shape[1]
    keep = min(B_REACH, S)
    kb_p, vb_p, mk_p_all, mv_p_all, kb_s, vb_s, va_s = [], [], [], [], [], [], []
    for i in range(DEPTH):
        j = i // 2
        mk_p, mv_p = mem_kv(mem_prompt, w_mem_kv[i])
        mk_p_all.append(mk_p)
        mv_p_all.append(mv_p)
        if i % 2 == 0:
            tok_p, qm_p, _ = sgu_branch(xp, w_in_a[j], sgu_ln_g[j], sgu_ln_b[j], sgu_w[j], sgu_b[j])
            tok_s, qm_s, v_rows = sgu_branch(xs, w_in_a[j], sgu_ln_g[j], sgu_ln_b[j], sgu_w[j], sgu_b[j])
            va_s.append(v_rows)
        else:
            q, k, v, qm_p = band_qkv(xp, w_in_b[j])
            tok_p = chunk_band_prompt(q, k, v, rel_bias[j])
            kb_p.append(k[:, S - keep:])
            vb_p.append(v[:, S - keep:])
            q, k, v, qm_s = band_qkv(xs, w_in_b[j])
            tok_s = chunk_band_sample(q, k, v, cache_b_k[j], cache_b_v[j], rel_bias[j])
            kb_s.append(k)
            vb_s.append(v)
        xp = residual_block(xp, tok_p, qm_p, mk_p, mv_p, w_out[i], ln1_g[i], ln1_b[i], ln2_g[i], ln2_b[i],
                            peer_wq[i], peer_keys[i], peer_u[i], peer_v[i])
        xs = residual_block(xs, tok_s, qm_s, cache_mem_k[i], cache_mem_v[i], w_out[i], ln1_g[i], ln1_b[i],
                            ln2_g[i], ln2_b[i], peer_wq[i], peer_keys[i], peer_u[i], peer_v[i])
    return (xp, xs, jnp.stack(kb_p), jnp.stack(vb_p), jnp.stack(mk_p_all), jnp.stack(mv_p_all),
            jnp.stack(kb_s), jnp.stack(vb_s), jnp.stack(va_s))
```

```python
import functools

import jax
import jax.numpy as jnp
from jax import lax
from jax.experimental import pallas as pl
from jax.experimental.pallas import tpu as pltpu

F32 = jnp.float32
BF16 = jnp.bfloat16

DEPTH = 2
CHUNK = 64
MEM_HEADS = 4
SGU_CHUNK = 128
SGU_GROUPS = 8
B_HD = 128
B_BAND_CHUNKS = 8
B_REACH = B_BAND_CHUNKS * CHUNK
REL_CLIP = 128
PEER_HEADS = 8
PEER_NKEYS = 128
PEER_TOPK = 16
PEER_TILE_KEYS = 4
ALPHA = (2 * DEPTH) ** 0.25
LN_EPS = 1e-5
MASK_VALUE = -1e30

VMEM_LIMIT_BYTES = 52 * 1024 * 1024
LANES = 128

_NT = (((1,), (1,)), ((), ()))
_TN = (((0,), (0,)), ((), ()))


def _params(*semantics):
    return pltpu.CompilerParams(dimension_semantics=semantics, vmem_limit_bytes=VMEM_LIMIT_BYTES)


def _layer_norm(x, g, b):
    mu = jnp.mean(x, axis=-1, keepdims=True)
    xc = x - mu
    var = jnp.mean(xc * xc, axis=-1, keepdims=True)
    return xc * lax.rsqrt(var + LN_EPS) * g + b


def _mm_kernel(a_ref, w_ref, o_ref, abf_ref):
    @pl.when(pl.program_id(1) == 0)
    def _():
        abf_ref[...] = a_ref[...].astype(BF16)

    o_ref[...] = jnp.dot(abf_ref[...], w_ref[...], preferred_element_type=F32).astype(o_ref.dtype)


def _matmul(a, w, tm=512, tn=512, out_dtype=F32):
    m, k = a.shape
    n = w.shape[1]
    tm, tn = min(tm, m), min(tn, n)
    assert m % tm == 0 and n % tn == 0
    return pl.pallas_call(
        _mm_kernel,
        grid=(m // tm, n // tn),
        in_specs=[pl.BlockSpec((tm, k), lambda i, j: (i, 0)),
                  pl.BlockSpec((k, tn), lambda i, j: (0, j))],
        out_specs=pl.BlockSpec((tm, tn), lambda i, j: (i, j)),
        out_shape=jax.ShapeDtypeStruct((m, n), out_dtype),
        scratch_shapes=[pltpu.VMEM((tm, k), BF16)],
        compiler_params=_params("parallel", "arbitrary"),
    )(a, w)


def _out_ln_kernel(tok_ref, mo_ref, w1_ref, w2_ref, x_ref, g_ref, b_ref, o_ref, obf_ref, *, tn, nj):
    j = pl.program_id(1)
    acc = jnp.dot(tok_ref[...], w1_ref[...], preferred_element_type=F32)
    acc += jnp.dot(mo_ref[...], w2_ref[...], preferred_element_type=F32)
    col = pl.multiple_of(j * tn, tn)
    o_ref[:, pl.ds(col, tn)] = ALPHA * x_ref[...] + acc

    @pl.when(j == nj - 1)
    def _():
        y = _layer_norm(o_ref[...], g_ref[...], b_ref[...])
        o_ref[...] = y
        obf_ref[...] = y.astype(BF16)


def _out_proj_ln(tok, mo, w_out, x, g, b, tm=256, tn=512):
    m, d = x.shape
    kt, km = tok.shape[1], mo.shape[1]
    assert kt % km == 0 and w_out.shape == (kt + km, d)
    tm, tn = min(tm, m), min(tn, d)
    assert m % tm == 0 and d % tn == 0
    nj = d // tn
    return pl.pallas_call(
        functools.partial(_out_ln_kernel, tn=tn, nj=nj),
        grid=(m // tm, nj),
        in_specs=[pl.BlockSpec((tm, kt), lambda i, j: (i, 0)),
                  pl.BlockSpec((tm, km), lambda i, j: (i, 0)),
                  pl.BlockSpec((kt, tn), lambda i, j: (0, j)),
                  pl.BlockSpec((km, tn), lambda i, j: (kt // km, j)),
                  pl.BlockSpec((tm, tn), lambda i, j: (i, j)),
                  pl.BlockSpec((1, d), lambda i, j: (0, 0)),
                  pl.BlockSpec((1, d), lambda i, j: (0, 0))],
        out_specs=[pl.BlockSpec((tm, d), lambda i, j: (i, 0)),
                   pl.BlockSpec((tm, d), lambda i, j: (i, 0))],
        out_shape=[jax.ShapeDtypeStruct((m, d), F32), jax.ShapeDtypeStruct((m, d), BF16)],
        compiler_params=_params("parallel", "arbitrary"),
    )(tok, mo, w_out, w_out, x, g.reshape(1, d), b.reshape(1, d))


def _add_ln_kernel(x_ref, y_ref, g_ref, b_ref, o_ref):
    o_ref[...] = _layer_norm(ALPHA * x_ref[...] + y_ref[...], g_ref[...], b_ref[...])


def _add_ln(x, y, g, b, tm=256):
    m, d = x.shape
    tm = min(tm, m)
    assert m % tm == 0
    row = pl.BlockSpec((tm, d), lambda i: (i, 0))
    vec = pl.BlockSpec((1, d), lambda i: (0, 0))
    return pl.pallas_call(
        _add_ln_kernel,
        grid=(m // tm,),
        in_specs=[row, row, vec, vec],
        out_specs=row,
        out_shape=jax.ShapeDtypeStruct((m, d), F32),
        compiler_params=_params("parallel"),
    )(x, y, g.reshape(1, d), b.reshape(1, d))


def _sgu_kernel(u_ref, v_ref, g_ref, b_ref, w_ref, bias_ref, tok_ref, *vn_refs, groups):
    n = u_ref.shape[0]
    gw = u_ref.shape[1] // groups
    vn = _layer_norm(jax.nn.gelu(v_ref[...]), g_ref[...], b_ref[...])
    if vn_refs:
        vn_refs[0][...] = vn
    causal = lax.broadcasted_iota(jnp.int32, (n, n), 1) <= lax.broadcasted_iota(jnp.int32, (n, n), 0)
    for g in range(groups):
        cols = slice(g * gw, (g + 1) * gw)
        w = jnp.where(causal, w_ref[g], 0.0).astype(BF16)
        s = jnp.dot(w, vn[:, cols].astype(BF16), preferred_element_type=F32) + bias_ref[:, cols]
        tok_ref[:, cols] = (jax.nn.gelu(u_ref[:, cols]) * s).astype(tok_ref.dtype)


def _sgu(h, tok_width, t, ln_g, ln_b, w_s, b_s, want_v):
    rows = h.shape[0]
    n = min(t, SGU_CHUNK)
    assert t % n == 0 and tok_width % SGU_GROUPS == 0
    gw = tok_width // SGU_GROUPS
    w = w_s[:, :n, :n]
    bias = jnp.repeat(b_s[:, :n].T, gw, axis=1)
    out_shape = [jax.ShapeDtypeStruct((rows, tok_width), BF16)]
    out_specs = [pl.BlockSpec((n, tok_width), lambda c: (c, 0))]
    if want_v:
        out_shape.append(jax.ShapeDtypeStruct((rows, tok_width), F32))
        out_specs.append(pl.BlockSpec((n, tok_width), lambda c: (c, 0)))
    vec = pl.BlockSpec((1, tok_width), lambda c: (0, 0))
    res = pl.pallas_call(
        functools.partial(_sgu_kernel, groups=SGU_GROUPS),
        grid=(rows // n,),
        in_specs=[pl.BlockSpec((n, tok_width), lambda c: (c, 0)),
                  pl.BlockSpec((n, tok_width), lambda c: (c, 1)),
                  vec, vec,
                  pl.BlockSpec((SGU_GROUPS, n, n), lambda c: (0, 0, 0)),
                  pl.BlockSpec((n, tok_width), lambda c: (0, 0))],
        out_specs=out_specs,
        out_shape=out_shape,
        compiler_params=_params("parallel"),
    )(h, h, ln_g.reshape(1, -1), ln_b.reshape(1, -1), w, bias)
    return res if want_v else (res[0], None)


def _mem_attn_kernel(q_ref, mk_ref, mv_ref, o_ref):
    hd = q_ref.shape[1] // MEM_HEADS
    for h in range(MEM_HEADS):
        cols = slice(h * hd, (h + 1) * hd)
        q = q_ref[:, cols].astype(BF16)
        s = lax.dot_general(q, mk_ref[:, cols].astype(BF16), _NT, preferred_element_type=F32) * (hd ** -0.5)
        p = jnp.exp(s - jnp.max(s, axis=-1, keepdims=True))
        l = jnp.sum(p, axis=-1, keepdims=True)
        o = jnp.dot(p.astype(BF16), mv_ref[:, cols].astype(BF16), preferred_element_type=F32)
        o_ref[:, cols] = (o / l).astype(o_ref.dtype)


def _mem_attn(h3, mem_width, mk, k_col, mv, v_col, tq=512):
    bsz, t, w = h3.shape
    assert w % mem_width == 0
    tq = min(tq, t)
    assert t % tq == 0
    n_mem = mk.shape[1]
    q_col = w // mem_width - 1
    return pl.pallas_call(
        _mem_attn_kernel,
        grid=(bsz, t // tq),
        in_specs=[pl.BlockSpec((None, tq, mem_width), lambda b, i: (b, i, q_col)),
                  pl.BlockSpec((None, n_mem, mem_width), lambda b, i: (b, 0, k_col)),
                  pl.BlockSpec((None, n_mem, mem_width), lambda b, i: (b, 0, v_col))],
        out_specs=pl.BlockSpec((None, tq, mem_width), lambda b, i: (b, i, 0)),
        out_shape=jax.ShapeDtypeStruct((bsz, t, mem_width), BF16),
        compiler_params=_params("parallel", "parallel"),
    )(h3, mk, mv)


def _band_kernel(q_ref, kp_ref, kc_ref, vp_ref, vc_ref, bp_ref, bc_ref, o_ref, *, heads, mask_first):
    scale = B_HD ** -0.5
    for h in range(heads):
        cols = slice(h * B_HD, (h + 1) * B_HD)
        q = q_ref[:, cols].astype(BF16)
        s0 = lax.dot_general(q, kp_ref[:, cols].astype(BF16), _NT, preferred_element_type=F32) * scale + bp_ref[h]
        s1 = lax.dot_general(q, kc_ref[:, cols].astype(BF16), _NT, preferred_element_type=F32) * scale + bc_ref[h]
        if mask_first:
            s0 = jnp.where(pl.program_id(2) == 0, MASK_VALUE, s0)
        m = jnp.maximum(jnp.max(s0, axis=-1, keepdims=True), jnp.max(s1, axis=-1, keepdims=True))
        p0 = jnp.exp(s0 - m)
        p1 = jnp.exp(s1 - m)
        l = jnp.sum(p0, axis=-1, keepdims=True) + jnp.sum(p1, axis=-1, keepdims=True)
        o = jnp.dot(p0.astype(BF16), vp_ref[:, cols].astype(BF16), preferred_element_type=F32)
        o += jnp.dot(p1.astype(BF16), vc_ref[:, cols].astype(BF16), preferred_element_type=F32)
        o_ref[:, cols] = (o / l).astype(o_ref.dtype)


def _rel_bias_matrix(rel_bias, tq, n_prev, n_cur, banded):
    i = jnp.arange(tq)[:, None]
    j = jnp.arange(n_prev + n_cur)[None, :]
    rel = jnp.clip(i - (j - n_prev), -REL_CLIP, REL_CLIP) + REL_CLIP
    bias = rel_bias[:, rel]
    if banded:
        visible = (j // CHUNK >= i // CHUNK) & (j // CHUNK <= i // CHUNK + B_BAND_CHUNKS)
        bias = jnp.where(visible[None], bias, MASK_VALUE)
    return bias


def _band_prompt(h3, tok_width, rel_bias, heads_per_step=2):
    bsz, s, _ = h3.shape
    tq = B_REACH
    assert s % tq == 0 and tok_width % (heads_per_step * B_HD) == 0
    hw = heads_per_step * B_HD
    ng = tok_width // hw
    bias = _rel_bias_matrix(rel_bias, tq, tq, tq, banded=True)
    blk = lambda idx: pl.BlockSpec((None, tq, hw), idx)
    prev = lambda i: jnp.maximum(i - 1, 0)
    return pl.pallas_call(
        functools.partial(_band_kernel, heads=heads_per_step, mask_first=True),
        grid=(ng, bsz, s // tq),
        in_specs=[blk(lambda g, b, i: (b, i, g)),
                  blk(lambda g, b, i: (b, prev(i), ng + g)),
                  blk(lambda g, b, i: (b, i, ng + g)),
                  blk(lambda g, b, i: (b, prev(i), 2 * ng + g)),
                  blk(lambda g, b, i: (b, i, 2 * ng + g)),
                  pl.BlockSpec((heads_per_step, tq, tq), lambda g, b, i: (g, 0, 0)),
                  pl.BlockSpec((heads_per_step, tq, tq), lambda g, b, i: (g, 0, 1))],
        out_specs=blk(lambda g, b, i: (b, i, g)),
        out_shape=jax.ShapeDtypeStruct((bsz, s, tok_width), BF16),
        compiler_params=_params("parallel", "parallel", "arbitrary"),
    )(h3, h3, h3, h3, h3, bias, bias)


def _band_sample(h3, tok_width, cache_k, cache_v, rel_bias, heads_per_step=4):
    bsz, t, _ = h3.shape
    cache_len = cache_k.shape[1]
    hw = heads_per_step * B_HD
    assert tok_width % hw == 0
    ng = tok_width // hw
    bias = _rel_bias_matrix(rel_bias, t, cache_len, t, banded=False)
    bias_cache, bias_new = bias[:, :, :cache_len], bias[:, :, cache_len:]
    new = lambda idx: pl.BlockSpec((None, t, hw), idx)
    old = lambda idx: pl.BlockSpec((None, cache_len, hw), idx)
    return pl.pallas_call(
        functools.partial(_band_kernel, heads=heads_per_step, mask_first=False),
        grid=(ng, bsz, 1),
        in_specs=[new(lambda g, b, i: (b, 0, g)),
                  old(lambda g, b, i: (b, 0, g)),
                  new(lambda g, b, i: (b, 0, ng + g)),
                  old(lambda g, b, i: (b, 0, g)),
                  new(lambda g, b, i: (b, 0, 2 * ng + g)),
                  pl.BlockSpec((heads_per_step, t, cache_len), lambda g, b, i: (g, 0, 0)),
                  pl.BlockSpec((heads_per_step, t, t), lambda g, b, i: (g, 0, 0))],
        out_specs=new(lambda g, b, i: (b, 0, g)),
        out_shape=jax.ShapeDtypeStruct((bsz, t, tok_width), BF16),
        compiler_params=_params("parallel", "parallel", "arbitrary"),
    )(h3, cache_k, h3, cache_v, h3, bias_cache, bias_new)


def _top16_rows(vals, order):
    rank = jnp.full(vals.shape, float(PEER_TOPK), F32)
    big = jnp.int32(2 ** 30)
    tops = []
    for r in range(PEER_TOPK):
        m = jnp.max(vals, axis=0, keepdims=True)
        first = jnp.min(jnp.where(vals == m, order, big), axis=0, keepdims=True)
        hit = order == first
        rank = jnp.where(hit, float(r), rank)
        vals = jnp.where(hit, -jnp.inf, vals)
        tops.append(m)
    return tops, rank


def _peer_select_kernel(q_ref, keys_ref, r2_ref, e2_ref, na_ref, f1_ref):
    nk, k = PEER_NKEYS, PEER_TOPK
    tq = q_ref.shape[0]
    row = lax.broadcasted_iota(jnp.int32, (nk, tq), 0)
    jrow = lax.broadcasted_iota(jnp.int32, (k, tq), 0)
    n_full = k // 2
    crow = lax.broadcasted_iota(jnp.int32, (n_full * k + k - n_full, tq), 0)
    flat = jnp.where(crow < n_full * k, crow, (crow - n_full * k + n_full) * k)
    for h in range(PEER_HEADS):
        tops, ranks, exps = [], [], []
        for p in range(2):
            c0 = (h * 2 + p) * nk
            qc = q_ref[:, c0:c0 + nk].astype(BF16)
            s = lax.dot_general(keys_ref[h, p], qc, _NT, preferred_element_type=F32)
            t, r = _top16_rows(s, row)
            tops.append(jnp.concatenate(t, axis=0))
            ranks.append(r)
            exps.append(jnp.exp(s - t[0]))
        t1, t2 = tops
        e1r = jnp.exp(t1 - t1[0:1])
        e2r = jnp.exp(t2 - t2[0:1])
        cand, wgt = [], []
        for i in range(n_full):
            c = t1[i:i + 1] + t2
            cand.append(jnp.where(jrow < k // (i + 1), c, -jnp.inf))
            wgt.append(e1r[i:i + 1] * e2r)
        cand.append(t1[n_full:] + t2[0:1])
        wgt.append(e1r[n_full:] * e2r[0:1])
        cand = jnp.concatenate(cand, axis=0)
        wgt = jnp.concatenate(wgt, axis=0)
        _, crank = _top16_rows(cand, flat)
        sel = jnp.where(crank < float(k), 1.0, 0.0)
        z = jnp.sum(sel * wgt, axis=0, keepdims=True)
        na = jnp.zeros((nk, tq), F32)
        for i in range(k):
            if i < n_full:
                n_i = jnp.sum(sel[i * k:(i + 1) * k], axis=0, keepdims=True)
            else:
                n_i = sel[n_full * k + i - n_full:n_full * k + i - n_full + 1]
            na = jnp.where(ranks[0] == float(i), n_i, na)
        r2_ref[h] = ranks[1]
        e2_ref[h] = exps[1]
        f1 = exps[0] / z
        ta = na_ref.shape[2]
        for g in range(nk // ta):
            na_ref[h, g] = na[g * ta:(g + 1) * ta]
            f1_ref[h, g] = f1[g * ta:(g + 1) * ta]


def _peer_select(q, keys, ta, tq=128):
    n = q.shape[0]
    assert n % tq == 0 and keys.shape == (PEER_HEADS, 2, PEER_NKEYS, q.shape[1] // (2 * PEER_HEADS))
    assert keys.shape[3] == PEER_NKEYS
    out = jax.ShapeDtypeStruct((PEER_HEADS, PEER_NKEYS, n), F32)
    spec = pl.BlockSpec((PEER_HEADS, PEER_NKEYS, tq), lambda i: (0, 0, i))
    out_g = jax.ShapeDtypeStruct((PEER_HEADS, PEER_NKEYS // ta, ta, n), F32)
    spec_g = pl.BlockSpec((PEER_HEADS, PEER_NKEYS // ta, ta, tq), lambda i: (0, 0, 0, i))
    return pl.pallas_call(
        _peer_select_kernel,
        grid=(n // tq,),
        in_specs=[pl.BlockSpec((tq, q.shape[1]), lambda i: (i, 0)),
                  pl.BlockSpec(keys.shape, lambda i: (0, 0, 0, 0))],
        out_specs=[spec, spec, spec_g, spec_g],
        out_shape=[out, out, out_g, out_g],
        compiler_params=_params("parallel"),
    )(q, keys)


def _peer_expert_kernel(x_ref, u_ref, v_ref, r2_ref, e2_ref, na_ref, f1_ref, o_ref, act_ref, a_ref):
    j = pl.program_id(1)
    te, tm = act_ref.shape
    nk = PEER_NKEYS

    @pl.when(j == 0)
    def _():
        o_ref[...] = jnp.zeros_like(o_ref)

    act_ref[...] = lax.dot_general(u_ref[...], x_ref[...], _NT, preferred_element_type=F32)
    for a in range(te // nk):
        for c in range(tm // LANES):
            lanes = slice(c * LANES, (c + 1) * LANES)
            gate = jnp.zeros((nk, LANES), F32)
            for h in range(PEER_HEADS):
                na = na_ref[h, a:a + 1, lanes]
                f1 = f1_ref[h, a:a + 1, lanes]
                gate += jnp.where(r2_ref[h, :, lanes] < na, e2_ref[h, :, lanes], 0.0) * f1
            rows = slice(a * nk, (a + 1) * nk)
            a_ref[rows, lanes] = (gate * jax.nn.gelu(act_ref[rows, lanes])).astype(BF16)
    o_ref[...] += lax.dot_general(a_ref[...], v_ref[...], _TN, preferred_element_type=F32)


def _peer_experts(x_bf, u, v, r2, e2, na, f1, tm=512):
    n, d = x_bf.shape
    ta = na.shape[2]
    te = ta * PEER_NKEYS
    n_exp = u.shape[0]
    tm = min(tm, n)
    assert n % tm == 0 and n_exp == PEER_NKEYS * PEER_NKEYS
    gate_spec = pl.BlockSpec((PEER_HEADS, PEER_NKEYS, tm), lambda i, j: (0, 0, i))
    row_spec = pl.BlockSpec((PEER_HEADS, None, ta, tm), lambda i, j: (0, j, 0, i))
    return pl.pallas_call(
        _peer_expert_kernel,
        grid=(n // tm, n_exp // te),
        in_specs=[pl.BlockSpec((tm, d), lambda i, j: (i, 0)),
                  pl.BlockSpec((te, d), lambda i, j: (j, 0)),
                  pl.BlockSpec((te, d), lambda i, j: (j, 0)),
                  gate_spec, gate_spec, row_spec, row_spec],
        out_specs=pl.BlockSpec((tm, d), lambda i, j: (i, 0)),
        out_shape=jax.ShapeDtypeStruct((n, d), F32),
        scratch_shapes=[pltpu.VMEM((te, tm), F32), pltpu.VMEM((te, tm), BF16)],
        compiler_params=_params("parallel", "arbitrary"),
    )(x_bf, u, v, r2, e2, na, f1)


def _residual_block(x, tok, h3, mk, k_col, mv, v_col, mem_width, w_out, ln1g, ln1b, ln2g, ln2b, wq, keys, u, v):
    rows = x.shape[0]
    mo = _mem_attn(h3, mem_width, mk, k_col, mv, v_col).reshape(rows, mem_width)
    x1, x1_bf = _out_proj_ln(tok, mo, w_out, x, ln1g, ln1b)
    q = _matmul(x1_bf, wq)
    r2, e2, na, f1 = _peer_select(q, keys, PEER_TILE_KEYS)
    y = _peer_experts(x1_bf, u, v, r2, e2, na, f1)
    return _add_ln(x1, y, ln2g, ln2b)


def kernel(x_prompt, x_sample, cache_b_k, cache_b_v, cache_mem_k, cache_mem_v, mem_prompt, w_in_a, sgu_ln_g, sgu_ln_b, sgu_w, sgu_b, w_in_b, rel_bias, w_mem_kv, w_out, ln1_g, ln1_b, ln2_g, ln2_b, peer_wq, peer_keys, peer_u, peer_v):
    bp, s, d = x_prompt.shape
    bs, t, _ = x_sample.shape
    depth = w_out.shape[0]
    assert depth == DEPTH
    n_mem = mem_prompt.shape[1]
    mem_width = w_mem_kv.shape[2] // 2
    tok_width = w_out.shape[1] - mem_width
    mem_hd = mem_width // MEM_HEADS
    b_heads = tok_width // B_HD
    keep = min(B_REACH, s)

    xp = x_prompt.reshape(bp * s, d)
    xs = x_sample.reshape(bs * t, d)
    kb_p, vb_p, mk_p_all, mv_p_all, kb_s, vb_s, va_s = [], [], [], [], [], [], []
    for i in range(depth):
        j = i // 2
        kv_p = _matmul(mem_prompt.reshape(bp * n_mem, d), w_mem_kv[i].astype(BF16)).reshape(bp, n_mem, 2 * mem_width)
        mk_p_all.append(kv_p[:, :, :mem_width].reshape(bp, n_mem, MEM_HEADS, mem_hd))
        mv_p_all.append(kv_p[:, :, mem_width:].reshape(bp, n_mem, MEM_HEADS, mem_hd))
        if i % 2 == 0:
            w_in = w_in_a[j].astype(BF16)
            h_p = _matmul(xp, w_in)
            h_s = _matmul(xs, w_in)
            tok_p, _ = _sgu(h_p, tok_width, s, sgu_ln_g[j], sgu_ln_b[j], sgu_w[j], sgu_b[j], want_v=False)
            tok_s, v_rows = _sgu(h_s, tok_width, t, sgu_ln_g[j], sgu_ln_b[j], sgu_w[j], sgu_b[j], want_v=True)
            va_s.append(v_rows.reshape(bs, t, tok_width))
            h_p = h_p.reshape(bp, s, -1)
            h_s = h_s.reshape(bs, t, -1)
        else:
            w_in = w_in_b[j].astype(BF16)
            h_p = _matmul(xp, w_in).reshape(bp, s, -1)
            h_s = _matmul(xs, w_in).reshape(bs, t, -1)
            tok_p = _band_prompt(h_p, tok_width, rel_bias[j]).reshape(bp * s, tok_width)
            ck = cache_b_k[j].reshape(bs, -1, tok_width)
            cv = cache_b_v[j].reshape(bs, -1, tok_width)
            tok_s = _band_sample(h_s, tok_width, ck, cv, rel_bias[j]).reshape(bs * t, tok_width)
            kb_p.append(h_p[:, s - keep:, tok_width:2 * tok_width].reshape(bp, keep, b_heads, B_HD))
            vb_p.append(h_p[:, s - keep:, 2 * tok_width:3 * tok_width].reshape(bp, keep, b_heads, B_HD))
            kb_s.append(h_s[:, :, tok_width:2 * tok_width].reshape(bs, t, b_heads, B_HD))
            vb_s.append(h_s[:, :, 2 * tok_width:3 * tok_width].reshape(bs, t, b_heads, B_HD))
        shared = (w_out[i].astype(BF16), ln1_g[i], ln1_b[i], ln2_g[i], ln2_b[i], peer_wq[i].astype(BF16),
                  peer_keys[i].astype(BF16), peer_u[i].astype(BF16), peer_v[i].astype(BF16))
        xp = _residual_block(xp, tok_p, h_p, kv_p, 0, kv_p, 1, mem_width, *shared)
        mk_s = cache_mem_k[i].reshape(bs, n_mem, mem_width)
        mv_s = cache_mem_v[i].reshape(bs, n_mem, mem_width)
        xs = _residual_block(xs, tok_s, h_s, mk_s, 0, mv_s, 0, mem_width, *shared)
    return (xp.reshape(bp, s, d), xs.reshape(bs, t, d), jnp.stack(kb_p), jnp.stack(vb_p),
            jnp.stack(mk_p_all), jnp.stack(mv_p_all), jnp.stack(kb_s), jnp.stack(vb_s), jnp.stack(va_s))
```

```python
import functools

import jax
import jax.numpy as jnp
from jax import lax
from jax.experimental import pallas as pl
from jax.experimental.pallas import tpu as pltpu

F32 = jnp.float32
BF16 = jnp.bfloat16

DEPTH = 2
CHUNK = 64
MEM_HEADS = 4
SGU_CHUNK = 128
SGU_GROUPS = 8
B_HD = 128
B_BAND_CHUNKS = 8
B_REACH = B_BAND_CHUNKS * CHUNK
REL_CLIP = 128
PEER_HEADS = 8
PEER_NKEYS = 128
PEER_TOPK = 16
PEER_TILE_KEYS = 4
ALPHA = (2 * DEPTH) ** 0.25
LN_EPS = 1e-5
MASK_VALUE = -1e30

VMEM_LIMIT_BYTES = 52 * 1024 * 1024
LANES = 128

_NT = (((1,), (1,)), ((), ()))
_TN = (((0,), (0,)), ((), ()))


def _params(*semantics):
    return pltpu.CompilerParams(dimension_semantics=semantics, vmem_limit_bytes=VMEM_LIMIT_BYTES)


def _layer_norm(x, g, b):
    mu = jnp.mean(x, axis=-1, keepdims=True)
    xc = x - mu
    var = jnp.mean(xc * xc, axis=-1, keepdims=True)
    return xc * lax.rsqrt(var + LN_EPS) * g + b


def _mm_kernel(a_ref, w_ref, o_ref, abf_ref):
    @pl.when(pl.program_id(1) == 0)
    def _():
        abf_ref[...] = a_ref[...].astype(BF16)

    o_ref[...] = jnp.dot(abf_ref[...], w_ref[...], preferred_element_type=F32).astype(o_ref.dtype)


def _mm_bf16_kernel(a_ref, w_ref, o_ref):
    o_ref[...] = jnp.dot(a_ref[...], w_ref[...], preferred_element_type=F32).astype(o_ref.dtype)


def _matmul(a, w, out_dtype=F32):
    m, k = a.shape
    n = w.shape[1]
    cast = a.dtype != BF16
    tm = min(512 if cast else 1024, m)
    tn = next(c for c in (1024, 512, 256, LANES) if n % c == 0)
    assert m % tm == 0
    return pl.pallas_call(
        _mm_kernel if cast else _mm_bf16_kernel,
        grid=(m // tm, n // tn),
        in_specs=[pl.BlockSpec((tm, k), lambda i, j: (i, 0)),
                  pl.BlockSpec((k, tn), lambda i, j: (0, j))],
        out_specs=pl.BlockSpec((tm, tn), lambda i, j: (i, j)),
        out_shape=jax.ShapeDtypeStruct((m, n), out_dtype),
        scratch_shapes=[pltpu.VMEM((tm, k), BF16)] if cast else [],
        compiler_params=_params("parallel", "arbitrary"),
    )(a, w)


def _out_ln_kernel(tok_ref, mo_ref, w1_ref, w2_ref, x_ref, g_ref, b_ref, o_ref, obf_ref, *, tn, nj):
    j = pl.program_id(1)
    acc = jnp.dot(tok_ref[...], w1_ref[...], preferred_element_type=F32)
    acc += jnp.dot(mo_ref[...], w2_ref[...], preferred_element_type=F32)
    col = pl.multiple_of(j * tn, tn)
    o_ref[:, pl.ds(col, tn)] = ALPHA * x_ref[...] + acc

    @pl.when(j == nj - 1)
    def _():
        y = _layer_norm(o_ref[...], g_ref[...], b_ref[...])
        o_ref[...] = y
        obf_ref[...] = y.astype(BF16)


def _out_proj_ln(tok, mo, w_out, x, g, b, tm=512, tn=512):
    m, d = x.shape
    kt, km = tok.shape[1], mo.shape[1]
    assert kt % km == 0 and w_out.shape == (kt + km, d)
    tm, tn = min(tm, m), min(tn, d)
    assert m % tm == 0 and d % tn == 0
    nj = d // tn
    return pl.pallas_call(
        functools.partial(_out_ln_kernel, tn=tn, nj=nj),
        grid=(m // tm, nj),
        in_specs=[pl.BlockSpec((tm, kt), lambda i, j: (i, 0)),
                  pl.BlockSpec((tm, km), lambda i, j: (i, 0)),
                  pl.BlockSpec((kt, tn), lambda i, j: (0, j)),
                  pl.BlockSpec((km, tn), lambda i, j: (kt // km, j)),
                  pl.BlockSpec((tm, tn), lambda i, j: (i, j)),
                  pl.BlockSpec((1, d), lambda i, j: (0, 0)),
                  pl.BlockSpec((1, d), lambda i, j: (0, 0))],
        out_specs=[pl.BlockSpec((tm, d), lambda i, j: (i, 0)),
                   pl.BlockSpec((tm, d), lambda i, j: (i, 0))],
        out_shape=[jax.ShapeDtypeStruct((m, d), F32), jax.ShapeDtypeStruct((m, d), BF16)],
        compiler_params=_params("parallel", "arbitrary"),
    )(tok, mo, w_out, w_out, x, g.reshape(1, d), b.reshape(1, d))


def _add_ln_kernel(x_ref, y_ref, g_ref, b_ref, o_ref, *obf_refs):
    out = _layer_norm(ALPHA * x_ref[...] + y_ref[...], g_ref[...], b_ref[...])
    o_ref[...] = out
    if obf_refs:
        obf_refs[0][...] = out.astype(BF16)


def _add_ln(x, y, g, b, want_bf16, tm=256):
    m, d = x.shape
    tm = min(tm, m)
    assert m % tm == 0
    row = pl.BlockSpec((tm, d), lambda i: (i, 0))
    vec = pl.BlockSpec((1, d), lambda i: (0, 0))
    n_out = 2 if want_bf16 else 1
    res = pl.pallas_call(
        _add_ln_kernel,
        grid=(m // tm,),
        in_specs=[row, row, vec, vec],
        out_specs=[row] * n_out,
        out_shape=[jax.ShapeDtypeStruct((m, d), F32), jax.ShapeDtypeStruct((m, d), BF16)][:n_out],
        compiler_params=_params("parallel"),
    )(x, y, g.reshape(1, d), b.reshape(1, d))
    return res if want_bf16 else (res[0], None)


def _sgu_kernel(u_ref, v_ref, g_ref, b_ref, w_ref, bias_ref, tok_ref, *vn_refs, groups):
    n = u_ref.shape[0]
    gw = u_ref.shape[1] // groups
    vn = _layer_norm(jax.nn.gelu(v_ref[...]), g_ref[...], b_ref[...])
    if vn_refs:
        vn_refs[0][...] = vn
    causal = lax.broadcasted_iota(jnp.int32, (n, n), 1) <= lax.broadcasted_iota(jnp.int32, (n, n), 0)
    for g in range(groups):
        cols = slice(g * gw, (g + 1) * gw)
        w = jnp.where(causal, w_ref[g], 0.0).astype(BF16)
        s = jnp.dot(w, vn[:, cols].astype(BF16), preferred_element_type=F32) + bias_ref[:, cols]
        tok_ref[:, cols] = (jax.nn.gelu(u_ref[:, cols]) * s).astype(tok_ref.dtype)


def _sgu(h, tok_width, t, ln_g, ln_b, w_s, b_s, want_v):
    rows = h.shape[0]
    n = min(t, SGU_CHUNK)
    assert t % n == 0 and tok_width % SGU_GROUPS == 0
    gw = tok_width // SGU_GROUPS
    w = w_s[:, :n, :n]
    bias = jnp.repeat(b_s[:, :n].T, gw, axis=1)
    out_shape = [jax.ShapeDtypeStruct((rows, tok_width), BF16)]
    out_specs = [pl.BlockSpec((n, tok_width), lambda c: (c, 0))]
    if want_v:
        out_shape.append(jax.ShapeDtypeStruct((rows, tok_width), F32))
        out_specs.append(pl.BlockSpec((n, tok_width), lambda c: (c, 0)))
    vec = pl.BlockSpec((1, tok_width), lambda c: (0, 0))
    res = pl.pallas_call(
        functools.partial(_sgu_kernel, groups=SGU_GROUPS),
        grid=(rows // n,),
        in_specs=[pl.BlockSpec((n, tok_width), lambda c: (c, 0)),
                  pl.BlockSpec((n, tok_width), lambda c: (c, 1)),
                  vec, vec,
                  pl.BlockSpec((SGU_GROUPS, n, n), lambda c: (0, 0, 0)),
                  pl.BlockSpec((n, tok_width), lambda c: (0, 0))],
        out_specs=out_specs,
        out_shape=out_shape,
        compiler_params=_params("parallel"),
    )(h, h, ln_g.reshape(1, -1), ln_b.reshape(1, -1), w, bias)
    return res if want_v else (res[0], None)


def _mem_attn_kernel(q_ref, mk_ref, mv_ref, o_ref):
    hd = q_ref.shape[1] // MEM_HEADS
    for h in range(MEM_HEADS):
        cols = slice(h * hd, (h + 1) * hd)
        q = q_ref[:, cols].astype(BF16)
        s = lax.dot_general(q, mk_ref[:, cols].astype(BF16), _NT, preferred_element_type=F32) * (hd ** -0.5)
        p = jnp.exp(s - jnp.max(s, axis=-1, keepdims=True))
        l = jnp.sum(p, axis=-1, keepdims=True)
        o = jnp.dot(p.astype(BF16), mv_ref[:, cols].astype(BF16), preferred_element_type=F32)
        o_ref[:, cols] = (o / l).astype(o_ref.dtype)


def _mem_attn(h3, mem_width, mk, k_col, mv, v_col, tq=512):
    bsz, t, w = h3.shape
    assert w % mem_width == 0
    tq = min(tq, t)
    assert t % tq == 0
    n_mem = mk.shape[1]
    q_col = w // mem_width - 1
    return pl.pallas_call(
        _mem_attn_kernel,
        grid=(bsz, t // tq),
        in_specs=[pl.BlockSpec((None, tq, mem_width), lambda b, i: (b, i, q_col)),
                  pl.BlockSpec((None, n_mem, mem_width), lambda b, i: (b, 0, k_col)),
                  pl.BlockSpec((None, n_mem, mem_width), lambda b, i: (b, 0, v_col))],
        out_specs=pl.BlockSpec((None, tq, mem_width), lambda b, i: (b, i, 0)),
        out_shape=jax.ShapeDtypeStruct((bsz, t, mem_width), BF16),
        compiler_params=_params("parallel", "parallel"),
    )(h3, mk, mv)


def _band_kernel(q_ref, kp_ref, kc_ref, vp_ref, vc_ref, t_ref, o_ref, bias_ref, *, heads, banded):
    tq, n_prev = q_ref.shape[0], kp_ref.shape[0]
    n_keys = bias_ref.shape[2]

    @pl.when((pl.program_id(1) == 0) & (pl.program_id(2) == 0))
    def _():
        row = lax.broadcasted_iota(jnp.int32, (tq, n_keys), 0)
        col = lax.broadcasted_iota(jnp.int32, (tq, n_keys), 1)
        ahead = col // CHUNK - row // CHUNK
        for h in range(heads):
            table = jnp.broadcast_to(t_ref[h], (tq, t_ref.shape[2]))
            bias = pltpu.roll(table, 0, 1, stride=1, stride_axis=0)[:, n_prev:n_prev + n_keys]
            if banded:
                bias = jnp.where(ahead >= 0, jnp.where(ahead <= B_BAND_CHUNKS, bias, MASK_VALUE), MASK_VALUE)
            bias_ref[h] = bias

    scale = B_HD ** -0.5
    for h in range(heads):
        cols = slice(h * B_HD, (h + 1) * B_HD)
        q = q_ref[:, cols].astype(BF16)
        s0 = lax.dot_general(q, kp_ref[:, cols].astype(BF16), _NT, preferred_element_type=F32) * scale
        s1 = lax.dot_general(q, kc_ref[:, cols].astype(BF16), _NT, preferred_element_type=F32) * scale
        s0 += bias_ref[h, :, :n_prev]
        s1 += bias_ref[h, :, n_prev:]
        if banded:
            s0 = jnp.where(pl.program_id(2) == 0, MASK_VALUE, s0)
        m = jnp.maximum(jnp.max(s0, axis=-1, keepdims=True), jnp.max(s1, axis=-1, keepdims=True))
        p0 = jnp.exp(s0 - m)
        p1 = jnp.exp(s1 - m)
        l = jnp.sum(p0, axis=-1, keepdims=True) + jnp.sum(p1, axis=-1, keepdims=True)
        o = jnp.dot(p0.astype(BF16), vp_ref[:, cols].astype(BF16), preferred_element_type=F32)
        o += jnp.dot(p1.astype(BF16), vc_ref[:, cols].astype(BF16), preferred_element_type=F32)
        o_ref[:, cols] = (o / l).astype(o_ref.dtype)


def _rel_bias_table(rel_bias, n_prev, length):
    lo = 2 * n_prev - REL_CLIP
    hi = lo + 2 * REL_CLIP + 1
    assert 0 <= lo and hi <= length
    heads = rel_bias.shape[0]
    table = jnp.concatenate([jnp.broadcast_to(rel_bias[:, -1:], (heads, lo)),
                             rel_bias[:, ::-1],
                             jnp.broadcast_to(rel_bias[:, :1], (heads, length - hi))], axis=1)
    return table[:, None, :]


def _band_attention(q3, prev_k, prev_v, rel_bias, tok_width, heads_per_step, banded):
    bsz, t, _ = q3.shape
    tq = min(t, B_REACH)
    n_prev = B_REACH if prev_k is None else prev_k.shape[1]
    hw = heads_per_step * B_HD
    assert t % tq == 0 and tok_width % hw == 0 and n_prev % LANES == 0 and tq <= n_prev
    ng = tok_width // hw
    n_keys = n_prev + tq
    length = pl.next_power_of_2(n_prev + n_keys)
    table = _rel_bias_table(rel_bias, n_prev, length)
    new = lambda idx: pl.BlockSpec((None, tq, hw), idx)
    old = lambda idx: pl.BlockSpec((None, n_prev, hw), idx)
    if prev_k is None:
        before = lambda i: jnp.maximum(i - 1, 0)
        prev_k = prev_v = q3
        k_spec = old(lambda g, b, i: (b, before(i), ng + g))
        v_spec = old(lambda g, b, i: (b, before(i), 2 * ng + g))
    else:
        k_spec = v_spec = old(lambda g, b, i: (b, 0, g))
    return pl.pallas_call(
        functools.partial(_band_kernel, heads=heads_per_step, banded=banded),
        grid=(ng, bsz, t // tq),
        in_specs=[new(lambda g, b, i: (b, i, g)),
                  k_spec,
                  new(lambda g, b, i: (b, i, ng + g)),
                  v_spec,
                  new(lambda g, b, i: (b, i, 2 * ng + g)),
                  pl.BlockSpec((heads_per_step, 1, length), lambda g, b, i: (g, 0, 0))],
        out_specs=new(lambda g, b, i: (b, i, g)),
        out_shape=jax.ShapeDtypeStruct((bsz, t, tok_width), BF16),
        scratch_shapes=[pltpu.VMEM((heads_per_step, tq, n_keys), F32)],
        compiler_params=_params("arbitrary", "arbitrary", "arbitrary"),
    )(q3, prev_k, q3, prev_v, q3, table)


def _top16_rows(vals, order):
    rank = jnp.full(vals.shape, float(PEER_TOPK), F32)
    big = jnp.int32(2 ** 30)
    tops = []
    for r in range(PEER_TOPK):
        m = jnp.max(vals, axis=0, keepdims=True)
        first = jnp.min(jnp.where(vals == m, order, big), axis=0, keepdims=True)
        hit = order == first
        rank = jnp.where(hit, float(r), rank)
        vals = jnp.where(hit, -jnp.inf, vals)
        tops.append(m)
    return tops, rank


def _peer_select_kernel(q_ref, keys_ref, r2_ref, e2_ref, na_ref, f1_ref):
    nk, k = PEER_NKEYS, PEER_TOPK
    tq = q_ref.shape[0]
    row = lax.broadcasted_iota(jnp.int32, (nk, tq), 0)
    jrow = lax.broadcasted_iota(jnp.int32, (k, tq), 0)
    n_full = k // 2
    crow = lax.broadcasted_iota(jnp.int32, (n_full * k + k - n_full, tq), 0)
    flat = jnp.where(crow < n_full * k, crow, (crow - n_full * k + n_full) * k)
    for h in range(PEER_HEADS):
        tops, ranks, exps = [], [], []
        for p in range(2):
            c0 = (h * 2 + p) * nk
            qc = q_ref[:, c0:c0 + nk].astype(BF16)
            s = lax.dot_general(keys_ref[h, p], qc, _NT, preferred_element_type=F32)
            t, r = _top16_rows(s, row)
            tops.append(jnp.concatenate(t, axis=0))
            ranks.append(r)
            exps.append(jnp.exp(s - t[0]))
        t1, t2 = tops
        e1r = jnp.exp(t1 - t1[0:1])
        e2r = jnp.exp(t2 - t2[0:1])
        cand, wgt = [], []
        for i in range(n_full):
            c = t1[i:i + 1] + t2
            cand.append(jnp.where(jrow < k // (i + 1), c, -jnp.inf))
            wgt.append(e1r[i:i + 1] * e2r)
        cand.append(t1[n_full:] + t2[0:1])
        wgt.append(e1r[n_full:] * e2r[0:1])
        cand = jnp.concatenate(cand, axis=0)
        wgt = jnp.concatenate(wgt, axis=0)
        _, crank = _top16_rows(cand, flat)
        sel = jnp.where(crank < float(k), 1.0, 0.0)
        z = jnp.sum(sel * wgt, axis=0, keepdims=True)
        na = jnp.zeros((nk, tq), F32)
        for i in range(k):
            if i < n_full:
                n_i = jnp.sum(sel[i * k:(i + 1) * k], axis=0, keepdims=True)
            else:
                n_i = sel[n_full * k + i - n_full:n_full * k + i - n_full + 1]
            na = jnp.where(ranks[0] == float(i), n_i, na)
        r2_ref[h] = ranks[1].astype(r2_ref.dtype)
        e2_ref[h] = exps[1].astype(e2_ref.dtype)
        f1 = exps[0] / z
        ta = na_ref.shape[2]
        for g in range(nk // ta):
            na_ref[h, g] = na[g * ta:(g + 1) * ta]
            f1_ref[h, g] = f1[g * ta:(g + 1) * ta]


def _peer_select(q, keys, ta, tq=128):
    n = q.shape[0]
    assert n % tq == 0 and keys.shape == (PEER_HEADS, 2, PEER_NKEYS, q.shape[1] // (2 * PEER_HEADS))
    assert keys.shape[3] == PEER_NKEYS
    out = jax.ShapeDtypeStruct((PEER_HEADS, PEER_NKEYS, n), BF16)
    spec = pl.BlockSpec((PEER_HEADS, PEER_NKEYS, tq), lambda i: (0, 0, i))
    out_g = jax.ShapeDtypeStruct((PEER_HEADS, PEER_NKEYS // ta, ta, n), F32)
    spec_g = pl.BlockSpec((PEER_HEADS, PEER_NKEYS // ta, ta, tq), lambda i: (0, 0, 0, i))
    return pl.pallas_call(
        _peer_select_kernel,
        grid=(n // tq,),
        in_specs=[pl.BlockSpec((tq, q.shape[1]), lambda i: (i, 0)),
                  pl.BlockSpec(keys.shape, lambda i: (0, 0, 0, 0))],
        out_specs=[spec, spec, spec_g, spec_g],
        out_shape=[out, out, out_g, out_g],
        compiler_params=_params("parallel"),
    )(q, keys)


def _peer_expert_kernel(x_ref, u_ref, v_ref, r2_ref, e2_ref, na_ref, f1_ref, o_ref, act_ref, a_ref):
    j = pl.program_id(1)
    te, tm = act_ref.shape
    nk = PEER_NKEYS

    @pl.when(j == 0)
    def _():
        o_ref[...] = jnp.zeros_like(o_ref)

    act_ref[...] = lax.dot_general(u_ref[...], x_ref[...], _NT, preferred_element_type=F32)
    zero = jnp.zeros((), BF16)
    for a in range(te // nk):
        rows = slice(a * nk, (a + 1) * nk)
        for c in range(tm // LANES):
            lanes = slice(c * LANES, (c + 1) * LANES)
            gate = jnp.zeros((nk, LANES), BF16)
            for h in range(PEER_HEADS):
                na = jnp.broadcast_to(na_ref[h, a:a + 1, lanes].astype(BF16), (nk, LANES))
                f1 = jnp.broadcast_to(f1_ref[h, a:a + 1, lanes].astype(BF16), (nk, LANES))
                gate += jnp.where(r2_ref[h, :, lanes] < na, e2_ref[h, :, lanes], zero) * f1
            a_ref[rows, lanes] = gate * jax.nn.gelu(act_ref[rows, lanes]).astype(BF16)
    o_ref[...] += lax.dot_general(a_ref[...], v_ref[...], _TN, preferred_element_type=F32)


def _peer_experts(x_bf, u, v, r2, e2, na, f1, tm=512):
    n, d = x_bf.shape
    ta = na.shape[2]
    te = ta * PEER_NKEYS
    n_exp = u.shape[0]
    tm = min(tm, n)
    assert n % tm == 0 and n_exp == PEER_NKEYS * PEER_NKEYS
    gate_spec = pl.BlockSpec((PEER_HEADS, PEER_NKEYS, tm), lambda i, j: (0, 0, i))
    row_spec = pl.BlockSpec((PEER_HEADS, None, ta, tm), lambda i, j: (0, j, 0, i))
    return pl.pallas_call(
        _peer_expert_kernel,
        grid=(n // tm, n_exp // te),
        in_specs=[pl.BlockSpec((tm, d), lambda i, j: (i, 0)),
                  pl.BlockSpec((te, d), lambda i, j: (j, 0)),
                  pl.BlockSpec((te, d), lambda i, j: (j, 0)),
                  gate_spec, gate_spec, row_spec, row_spec],
        out_specs=pl.BlockSpec((tm, d), lambda i, j: (i, 0)),
        out_shape=jax.ShapeDtypeStruct((n, d), F32),
        scratch_shapes=[pltpu.VMEM((te, tm), F32), pltpu.VMEM((te, tm), BF16)],
        compiler_params=_params("parallel", "arbitrary"),
    )(x_bf, u, v, r2, e2, na, f1)


def _residual_block(x, tok, h3, mk, k_col, mv, v_col, mem_width, want_bf16,
                    w_out, ln1g, ln1b, ln2g, ln2b, wq, keys, u, v):
    rows = x.shape[0]
    mo = _mem_attn(h3, mem_width, mk, k_col, mv, v_col).reshape(rows, mem_width)
    x1, x1_bf = _out_proj_ln(tok, mo, w_out, x, ln1g, ln1b)
    q = _matmul(x1_bf, wq)
    r2, e2, na, f1 = _peer_select(q, keys, PEER_TILE_KEYS)
    y = _peer_experts(x1_bf, u, v, r2, e2, na, f1)
    return _add_ln(x1, y, ln2g, ln2b, want_bf16)


def kernel(x_prompt, x_sample, cache_b_k, cache_b_v, cache_mem_k, cache_mem_v, mem_prompt, w_in_a, sgu_ln_g, sgu_ln_b, sgu_w, sgu_b, w_in_b, rel_bias, w_mem_kv, w_out, ln1_g, ln1_b, ln2_g, ln2_b, peer_wq, peer_keys, peer_u, peer_v):
    bp, s, d = x_prompt.shape
    bs, t, _ = x_sample.shape
    depth = w_out.shape[0]
    assert depth == DEPTH
    n_mem = mem_prompt.shape[1]
    mem_width = w_mem_kv.shape[2] // 2
    tok_width = w_out.shape[1] - mem_width
    mem_hd = mem_width // MEM_HEADS
    b_heads = tok_width // B_HD
    keep = min(B_REACH, s)

    xp = x_prompt.reshape(bp * s, d)
    xs = x_sample.reshape(bs * t, d)
    xp_in, xs_in = xp, xs
    kb_p, vb_p, mk_p_all, mv_p_all, kb_s, vb_s, va_s = [], [], [], [], [], [], []
    for i in range(depth):
        j = i // 2
        kv_p = _matmul(mem_prompt.reshape(bp * n_mem, d), w_mem_kv[i].astype(BF16)).reshape(bp, n_mem, 2 * mem_width)
        mk_p_all.append(kv_p[:, :, :mem_width].reshape(bp, n_mem, MEM_HEADS, mem_hd))
        mv_p_all.append(kv_p[:, :, mem_width:].reshape(bp, n_mem, MEM_HEADS, mem_hd))
        if i % 2 == 0:
            w_in = w_in_a[j].astype(BF16)
            h_p = _matmul(xp_in, w_in)
            h_s = _matmul(xs_in, w_in)
            tok_p, _ = _sgu(h_p, tok_width, s, sgu_ln_g[j], sgu_ln_b[j], sgu_w[j], sgu_b[j], want_v=False)
            tok_s, v_rows = _sgu(h_s, tok_width, t, sgu_ln_g[j], sgu_ln_b[j], sgu_w[j], sgu_b[j], want_v=True)
            va_s.append(v_rows.reshape(bs, t, tok_width))
            h_p = h_p.reshape(bp, s, -1)
            h_s = h_s.reshape(bs, t, -1)
        else:
            w_in = w_in_b[j].astype(BF16)
            h_p = _matmul(xp_in, w_in).reshape(bp, s, -1)
            h_s = _matmul(xs_in, w_in).reshape(bs, t, -1)
            tok_p = _band_attention(h_p, None, None, rel_bias[j], tok_width, 2, banded=True)
            tok_p = tok_p.reshape(bp * s, tok_width)
            ck = cache_b_k[j].reshape(bs, -1, tok_width)
            cv = cache_b_v[j].reshape(bs, -1, tok_width)
            tok_s = _band_attention(h_s, ck, cv, rel_bias[j], tok_width, 4, banded=False)
            tok_s = tok_s.reshape(bs * t, tok_width)
            kb_p.append(h_p[:, s - keep:, tok_width:2 * tok_width].reshape(bp, keep, b_heads, B_HD))
            vb_p.append(h_p[:, s - keep:, 2 * tok_width:3 * tok_width].reshape(bp, keep, b_heads, B_HD))
            kb_s.append(h_s[:, :, tok_width:2 * tok_width].reshape(bs, t, b_heads, B_HD))
            vb_s.append(h_s[:, :, 2 * tok_width:3 * tok_width].reshape(bs, t, b_heads, B_HD))
        shared = (w_out[i].astype(BF16), ln1_g[i], ln1_b[i], ln2_g[i], ln2_b[i], peer_wq[i].astype(BF16),
                  peer_keys[i].astype(BF16), peer_u[i].astype(BF16), peer_v[i].astype(BF16))
        more = i + 1 < depth
        xp, xp_in = _residual_block(xp, tok_p, h_p, kv_p, 0, kv_p, 1, mem_width, more, *shared)
        mk_s = cache_mem_k[i].reshape(bs, n_mem, mem_width)
        mv_s = cache_mem_v[i].reshape(bs, n_mem, mem_width)
        xs, xs_in = _residual_block(xs, tok_s, h_s, mk_s, 0, mv_s, 0, mem_width, more, *shared)
    return (xp.reshape(bp, s, d), xs.reshape(bs, t, d), jnp.stack(kb_p), jnp.stack(vb_p),
            jnp.stack(mk_p_all), jnp.stack(mv_p_all), jnp.stack(kb_s), jnp.stack(vb_s), jnp.stack(va_s))
```

```python
import functools

import jax
import jax.numpy as jnp
from jax import lax
from jax.experimental import pallas as pl
from jax.experimental.pallas import tpu as pltpu

F32 = jnp.float32
BF16 = jnp.bfloat16

DEPTH = 2
CHUNK = 64
MEM_HEADS = 4
SGU_CHUNK = 128
SGU_GROUPS = 8
B_HD = 128
B_BAND_CHUNKS = 8
B_REACH = B_BAND_CHUNKS * CHUNK
REL_CLIP = 128
PEER_HEADS = 8
PEER_NKEYS = 128
PEER_TOPK = 16
PEER_TILE_KEYS = 4
ALPHA = (2 * DEPTH) ** 0.25
LN_EPS = 1e-5
MASK_VALUE = -1e30

VMEM_LIMIT_BYTES = 52 * 1024 * 1024
LANES = 128

_NT = (((1,), (1,)), ((), ()))
_TN = (((0,), (0,)), ((), ()))


def _params(*semantics):
    return pltpu.CompilerParams(dimension_semantics=semantics, vmem_limit_bytes=VMEM_LIMIT_BYTES)


def _layer_norm(x, g, b):
    mu = jnp.mean(x, axis=-1, keepdims=True)
    xc = x - mu
    var = jnp.mean(xc * xc, axis=-1, keepdims=True)
    return xc * lax.rsqrt(var + LN_EPS) * g + b


CAST_BLOCK_BYTES = 8 * 1024 * 1024


def _cast_kernel(w_ref, o_ref):
    o_ref[...] = w_ref[...].astype(o_ref.dtype)


def _layer_weight_bf16(w, layer):
    _, r, c = w.shape
    rows = r
    while rows * c * w.dtype.itemsize > CAST_BLOCK_BYTES and rows % 16 == 0:
        rows //= 2
    return pl.pallas_call(
        _cast_kernel,
        grid=(r // rows,),
        in_specs=[pl.BlockSpec((None, rows, c), lambda i: (layer, i, 0))],
        out_specs=pl.BlockSpec((rows, c), lambda i: (i, 0)),
        out_shape=jax.ShapeDtypeStruct((r, c), BF16),
        compiler_params=_params("parallel"),
    )(w)


def _mm_kernel(a_ref, w_ref, o_ref, abf_ref):
    @pl.when(pl.program_id(1) == 0)
    def _():
        abf_ref[...] = a_ref[...].astype(BF16)

    o_ref[...] = jnp.dot(abf_ref[...], w_ref[...], preferred_element_type=F32).astype(o_ref.dtype)


def _mm_bf16_kernel(a_ref, w_ref, o_ref):
    o_ref[...] = jnp.dot(a_ref[...], w_ref[...], preferred_element_type=F32).astype(o_ref.dtype)


def _matmul(a, w, out_dtype=F32):
    m, k = a.shape
    n = w.shape[1]
    cast = a.dtype != BF16
    tm = min(512 if cast else 1024, m)
    tn = next(c for c in (1024, 512, 256, LANES) if n % c == 0)
    assert m % tm == 0
    return pl.pallas_call(
        _mm_kernel if cast else _mm_bf16_kernel,
        grid=(m // tm, n // tn),
        in_specs=[pl.BlockSpec((tm, k), lambda i, j: (i, 0)),
                  pl.BlockSpec((k, tn), lambda i, j: (0, j))],
        out_specs=pl.BlockSpec((tm, tn), lambda i, j: (i, j)),
        out_shape=jax.ShapeDtypeStruct((m, n), out_dtype),
        scratch_shapes=[pltpu.VMEM((tm, k), BF16)] if cast else [],
        compiler_params=_params("parallel", "arbitrary"),
    )(a, w)


def _out_ln_kernel(tok_ref, mo_ref, w1_ref, w2_ref, x_ref, g_ref, b_ref, o_ref, obf_ref, *, tn, nj):
    j = pl.program_id(1)
    acc = jnp.dot(tok_ref[...], w1_ref[...], preferred_element_type=F32)
    acc += jnp.dot(mo_ref[...], w2_ref[...], preferred_element_type=F32)
    col = pl.multiple_of(j * tn, tn)
    o_ref[:, pl.ds(col, tn)] = ALPHA * x_ref[...] + acc

    @pl.when(j == nj - 1)
    def _():
        y = _layer_norm(o_ref[...], g_ref[...], b_ref[...])
        o_ref[...] = y
        obf_ref[...] = y.astype(BF16)


def _out_proj_ln(tok, mo, w_out, x, g, b, tm=512, tn=512):
    m, d = x.shape
    kt, km = tok.shape[1], mo.shape[1]
    assert kt % km == 0 and w_out.shape == (kt + km, d)
    tm, tn = min(tm, m), min(tn, d)
    assert m % tm == 0 and d % tn == 0
    nj = d // tn
    return pl.pallas_call(
        functools.partial(_out_ln_kernel, tn=tn, nj=nj),
        grid=(m // tm, nj),
        in_specs=[pl.BlockSpec((tm, kt), lambda i, j: (i, 0)),
                  pl.BlockSpec((tm, km), lambda i, j: (i, 0)),
                  pl.BlockSpec((kt, tn), lambda i, j: (0, j)),
                  pl.BlockSpec((km, tn), lambda i, j: (kt // km, j)),
                  pl.BlockSpec((tm, tn), lambda i, j: (i, j)),
                  pl.BlockSpec((1, d), lambda i, j: (0, 0)),
                  pl.BlockSpec((1, d), lambda i, j: (0, 0))],
        out_specs=[pl.BlockSpec((tm, d), lambda i, j: (i, 0)),
                   pl.BlockSpec((tm, d), lambda i, j: (i, 0))],
        out_shape=[jax.ShapeDtypeStruct((m, d), F32), jax.ShapeDtypeStruct((m, d), BF16)],
        compiler_params=_params("parallel", "arbitrary"),
    )(tok, mo, w_out, w_out, x, g.reshape(1, d), b.reshape(1, d))


def _add_ln_kernel(x_ref, y_ref, g_ref, b_ref, o_ref, *obf_refs):
    out = _layer_norm(ALPHA * x_ref[...] + y_ref[...], g_ref[...], b_ref[...])
    o_ref[...] = out
    if obf_refs:
        obf_refs[0][...] = out.astype(BF16)


def _add_ln(x, y, g, b, want_bf16, tm=256):
    m, d = x.shape
    tm = min(tm, m)
    assert m % tm == 0
    row = pl.BlockSpec((tm, d), lambda i: (i, 0))
    vec = pl.BlockSpec((1, d), lambda i: (0, 0))
    n_out = 2 if want_bf16 else 1
    res = pl.pallas_call(
        _add_ln_kernel,
        grid=(m // tm,),
        in_specs=[row, row, vec, vec],
        out_specs=[row] * n_out,
        out_shape=[jax.ShapeDtypeStruct((m, d), F32), jax.ShapeDtypeStruct((m, d), BF16)][:n_out],
        compiler_params=_params("parallel"),
    )(x, y, g.reshape(1, d), b.reshape(1, d))
    return res if want_bf16 else (res[0], None)


def _sgu_kernel(u_ref, v_ref, g_ref, b_ref, w_ref, bias_ref, tok_ref, *vn_refs, groups):
    n = u_ref.shape[0]
    gw = u_ref.shape[1] // groups
    vn = _layer_norm(jax.nn.gelu(v_ref[...]), g_ref[...], b_ref[...])
    if vn_refs:
        vn_refs[0][...] = vn
    causal = lax.broadcasted_iota(jnp.int32, (n, n), 1) <= lax.broadcasted_iota(jnp.int32, (n, n), 0)
    for g in range(groups):
        cols = slice(g * gw, (g + 1) * gw)
        w = jnp.where(causal, w_ref[g], 0.0).astype(BF16)
        s = jnp.dot(w, vn[:, cols].astype(BF16), preferred_element_type=F32) + bias_ref[:, cols]
        tok_ref[:, cols] = (jax.nn.gelu(u_ref[:, cols]) * s).astype(tok_ref.dtype)


def _sgu(h, tok_width, t, ln_g, ln_b, w_s, b_s, want_v):
    rows = h.shape[0]
    n = min(t, SGU_CHUNK)
    assert t % n == 0 and tok_width % SGU_GROUPS == 0
    gw = tok_width // SGU_GROUPS
    w = w_s[:, :n, :n]
    bias = jnp.repeat(b_s[:, :n].T, gw, axis=1)
    out_shape = [jax.ShapeDtypeStruct((rows, tok_width), BF16)]
    out_specs = [pl.BlockSpec((n, tok_width), lambda c: (c, 0))]
    if want_v:
        out_shape.append(jax.ShapeDtypeStruct((rows, tok_width), F32))
        out_specs.append(pl.BlockSpec((n, tok_width), lambda c: (c, 0)))
    vec = pl.BlockSpec((1, tok_width), lambda c: (0, 0))
    res = pl.pallas_call(
        functools.partial(_sgu_kernel, groups=SGU_GROUPS),
        grid=(rows // n,),
        in_specs=[pl.BlockSpec((n, tok_width), lambda c: (c, 0)),
                  pl.BlockSpec((n, tok_width), lambda c: (c, 1)),
                  vec, vec,
                  pl.BlockSpec((SGU_GROUPS, n, n), lambda c: (0, 0, 0)),
                  pl.BlockSpec((n, tok_width), lambda c: (0, 0))],
        out_specs=out_specs,
        out_shape=out_shape,
        compiler_params=_params("parallel"),
    )(h, h, ln_g.reshape(1, -1), ln_b.reshape(1, -1), w, bias)
    return res if want_v else (res[0], None)


def _mem_attn_kernel(q_ref, mk_ref, mv_ref, o_ref):
    hd = q_ref.shape[1] // MEM_HEADS
    for h in range(MEM_HEADS):
        cols = slice(h * hd, (h + 1) * hd)
        q = q_ref[:, cols].astype(BF16)
        s = lax.dot_general(q, mk_ref[:, cols].astype(BF16), _NT, preferred_element_type=F32) * (hd ** -0.5)
        p = jnp.exp(s - jnp.max(s, axis=-1, keepdims=True))
        l = jnp.sum(p, axis=-1, keepdims=True)
        o = jnp.dot(p.astype(BF16), mv_ref[:, cols].astype(BF16), preferred_element_type=F32)
        o_ref[:, cols] = (o / l).astype(o_ref.dtype)


def _mem_attn(h3, mem_width, mk, k_col, mv, v_col, tq=512):
    bsz, t, w = h3.shape
    assert w % mem_width == 0
    tq = min(tq, t)
    assert t % tq == 0
    n_mem = mk.shape[1]
    q_col = w // mem_width - 1
    return pl.pallas_call(
        _mem_attn_kernel,
        grid=(bsz, t // tq),
        in_specs=[pl.BlockSpec((None, tq, mem_width), lambda b, i: (b, i, q_col)),
                  pl.BlockSpec((None, n_mem, mem_width), lambda b, i: (b, 0, k_col)),
                  pl.BlockSpec((None, n_mem, mem_width), lambda b, i: (b, 0, v_col))],
        out_specs=pl.BlockSpec((None, tq, mem_width), lambda b, i: (b, i, 0)),
        out_shape=jax.ShapeDtypeStruct((bsz, t, mem_width), BF16),
        compiler_params=_params("parallel", "parallel"),
    )(h3, mk, mv)


def _band_kernel(q_ref, kp_ref, kc_ref, vp_ref, vc_ref, t_ref, o_ref, bias_ref, *, heads, banded):
    tq, n_prev = q_ref.shape[0], kp_ref.shape[0]
    n_keys = bias_ref.shape[2]

    @pl.when((pl.program_id(1) == 0) & (pl.program_id(2) == 0))
    def _():
        row = lax.broadcasted_iota(jnp.int32, (tq, n_keys), 0)
        col = lax.broadcasted_iota(jnp.int32, (tq, n_keys), 1)
        ahead = col // CHUNK - row // CHUNK
        for h in range(heads):
            table = jnp.broadcast_to(t_ref[h], (tq, t_ref.shape[2]))
            bias = pltpu.roll(table, 0, 1, stride=1, stride_axis=0)[:, n_prev:n_prev + n_keys]
            if banded:
                bias = jnp.where(ahead >= 0, jnp.where(ahead <= B_BAND_CHUNKS, bias, MASK_VALUE), MASK_VALUE)
            bias_ref[h] = bias

    scale = B_HD ** -0.5
    for h in range(heads):
        cols = slice(h * B_HD, (h + 1) * B_HD)
        q = q_ref[:, cols].astype(BF16)
        s0 = lax.dot_general(q, kp_ref[:, cols].astype(BF16), _NT, preferred_element_type=F32) * scale
        s1 = lax.dot_general(q, kc_ref[:, cols].astype(BF16), _NT, preferred_element_type=F32) * scale
        s0 += bias_ref[h, :, :n_prev]
        s1 += bias_ref[h, :, n_prev:]
        if banded:
            s0 = jnp.where(pl.program_id(2) == 0, MASK_VALUE, s0)
        m = jnp.maximum(jnp.max(s0, axis=-1, keepdims=True), jnp.max(s1, axis=-1, keepdims=True))
        p0 = jnp.exp(s0 - m)
        p1 = jnp.exp(s1 - m)
        l = jnp.sum(p0, axis=-1, keepdims=True) + jnp.sum(p1, axis=-1, keepdims=True)
        o = jnp.dot(p0.astype(BF16), vp_ref[:, cols].astype(BF16), preferred_element_type=F32)
        o += jnp.dot(p1.astype(BF16), vc_ref[:, cols].astype(BF16), preferred_element_type=F32)
        o_ref[:, cols] = (o / l).astype(o_ref.dtype)


def _rel_bias_table(rel_bias, n_prev, length):
    lo = 2 * n_prev - REL_CLIP
    hi = lo + 2 * REL_CLIP + 1
    assert 0 <= lo and hi <= length
    heads = rel_bias.shape[0]
    table = jnp.concatenate([jnp.broadcast_to(rel_bias[:, -1:], (heads, lo)),
                             rel_bias[:, ::-1],
                             jnp.broadcast_to(rel_bias[:, :1], (heads, length - hi))], axis=1)
    return table[:, None, :]


def _band_attention(q3, prev_k, prev_v, rel_bias, tok_width, heads_per_step, banded):
    bsz, t, _ = q3.shape
    tq = min(t, B_REACH)
    n_prev = B_REACH if prev_k is None else prev_k.shape[1]
    hw = heads_per_step * B_HD
    assert t % tq == 0 and tok_width % hw == 0 and n_prev % LANES == 0 and tq <= n_prev
    ng = tok_width // hw
    n_keys = n_prev + tq
    length = pl.next_power_of_2(n_prev + n_keys)
    table = _rel_bias_table(rel_bias, n_prev, length)
    new = lambda idx: pl.BlockSpec((None, tq, hw), idx)
    old = lambda idx: pl.BlockSpec((None, n_prev, hw), idx)
    if prev_k is None:
        before = lambda i: jnp.maximum(i - 1, 0)
        prev_k = prev_v = q3
        k_spec = old(lambda g, b, i: (b, before(i), ng + g))
        v_spec = old(lambda g, b, i: (b, before(i), 2 * ng + g))
    else:
        k_spec = v_spec = old(lambda g, b, i: (b, 0, g))
    return pl.pallas_call(
        functools.partial(_band_kernel, heads=heads_per_step, banded=banded),
        grid=(ng, bsz, t // tq),
        in_specs=[new(lambda g, b, i: (b, i, g)),
                  k_spec,
                  new(lambda g, b, i: (b, i, ng + g)),
                  v_spec,
                  new(lambda g, b, i: (b, i, 2 * ng + g)),
                  pl.BlockSpec((heads_per_step, 1, length), lambda g, b, i: (g, 0, 0))],
        out_specs=new(lambda g, b, i: (b, i, g)),
        out_shape=jax.ShapeDtypeStruct((bsz, t, tok_width), BF16),
        scratch_shapes=[pltpu.VMEM((heads_per_step, tq, n_keys), F32)],
        compiler_params=_params("arbitrary", "arbitrary", "arbitrary"),
    )(q3, prev_k, q3, prev_v, q3, table)


def _top16_rows(vals, order):
    rank = jnp.full(vals.shape, float(PEER_TOPK), F32)
    big = jnp.int32(2 ** 30)
    tops = []
    for r in range(PEER_TOPK):
        m = jnp.max(vals, axis=0, keepdims=True)
        first = jnp.min(jnp.where(vals == m, order, big), axis=0, keepdims=True)
        hit = order == first
        rank = jnp.where(hit, float(r), rank)
        vals = jnp.where(hit, -jnp.inf, vals)
        tops.append(m)
    return tops, rank


def _peer_select_kernel(q_ref, keys_ref, r2_ref, e2_ref, na_ref, f1_ref):
    nk, k = PEER_NKEYS, PEER_TOPK
    tq = q_ref.shape[0]
    row = lax.broadcasted_iota(jnp.int32, (nk, tq), 0)
    jrow = lax.broadcasted_iota(jnp.int32, (k, tq), 0)
    n_full = k // 2
    crow = lax.broadcasted_iota(jnp.int32, (n_full * k + k - n_full, tq), 0)
    flat = jnp.where(crow < n_full * k, crow, (crow - n_full * k + n_full) * k)
    for h in range(PEER_HEADS):
        tops, ranks, exps = [], [], []
        for p in range(2):
            c0 = (h * 2 + p) * nk
            qc = q_ref[:, c0:c0 + nk].astype(BF16)
            s = lax.dot_general(keys_ref[h, p], qc, _NT, preferred_element_type=F32)
            t, r = _top16_rows(s, row)
            tops.append(jnp.concatenate(t, axis=0))
            ranks.append(r)
            exps.append(jnp.exp(s - t[0]))
        t1, t2 = tops
        e1r = jnp.exp(t1 - t1[0:1])
        e2r = jnp.exp(t2 - t2[0:1])
        cand, wgt = [], []
        for i in range(n_full):
            c = t1[i:i + 1] + t2
            cand.append(jnp.where(jrow < k // (i + 1), c, -jnp.inf))
            wgt.append(e1r[i:i + 1] * e2r)
        cand.append(t1[n_full:] + t2[0:1])
        wgt.append(e1r[n_full:] * e2r[0:1])
        cand = jnp.concatenate(cand, axis=0)
        wgt = jnp.concatenate(wgt, axis=0)
        _, crank = _top16_rows(cand, flat)
        sel = jnp.where(crank < float(k), 1.0, 0.0)
        z = jnp.sum(sel * wgt, axis=0, keepdims=True)
        na = jnp.zeros((nk, tq), F32)
        for i in range(k):
            if i < n_full:
                n_i = jnp.sum(sel[i * k:(i + 1) * k], axis=0, keepdims=True)
            else:
                n_i = sel[n_full * k + i - n_full:n_full * k + i - n_full + 1]
            na = jnp.where(ranks[0] == float(i), n_i, na)
        r2_ref[h] = pltpu.bitcast(ranks[1].astype(BF16), jnp.uint32)
        e2_ref[h] = pltpu.bitcast(exps[1].astype(BF16), jnp.uint32)
        f1 = exps[0] / z
        ta = na_ref.shape[2]
        for g in range(nk // ta):
            na_ref[h, g] = na[g * ta:(g + 1) * ta]
            f1_ref[h, g] = f1[g * ta:(g + 1) * ta]


def _peer_select(q, keys, ta, tq=128):
    n = q.shape[0]
    assert n % tq == 0 and keys.shape == (PEER_HEADS, 2, PEER_NKEYS, q.shape[1] // (2 * PEER_HEADS))
    assert keys.shape[3] == PEER_NKEYS
    out = jax.ShapeDtypeStruct((PEER_HEADS, PEER_NKEYS // 2, n), jnp.uint32)
    spec = pl.BlockSpec((PEER_HEADS, PEER_NKEYS // 2, tq), lambda i: (0, 0, i))
    out_g = jax.ShapeDtypeStruct((PEER_HEADS, PEER_NKEYS // ta, ta, n), F32)
    spec_g = pl.BlockSpec((PEER_HEADS, PEER_NKEYS // ta, ta, tq), lambda i: (0, 0, 0, i))
    return pl.pallas_call(
        _peer_select_kernel,
        grid=(n // tq,),
        in_specs=[pl.BlockSpec((tq, q.shape[1]), lambda i: (i, 0)),
                  pl.BlockSpec(keys.shape, lambda i: (0, 0, 0, 0))],
        out_specs=[spec, spec, spec_g, spec_g],
        out_shape=[out, out, out_g, out_g],
        compiler_params=_params("parallel"),
    )(q, keys)


def _peer_expert_kernel(x_ref, u_ref, v_ref, r2_ref, e2_ref, na_ref, f1_ref, o_ref, act_ref, a_ref):
    j = pl.program_id(1)
    te, tm = act_ref.shape
    nk = PEER_NKEYS

    @pl.when(j == 0)
    def _():
        o_ref[...] = jnp.zeros_like(o_ref)

    act_ref[...] = lax.dot_general(u_ref[...], x_ref[...], _NT, preferred_element_type=F32)
    zero = jnp.zeros((), BF16)

    def rows_bf16(row):
        bits = pltpu.bitcast(row.astype(BF16).astype(F32), jnp.uint32) & jnp.uint32(0xFFFF0000)
        return pltpu.bitcast(jnp.broadcast_to(bits | (bits >> 16), (nk // 2, LANES)), BF16)

    for a in range(te // nk):
        rows = slice(a * nk, (a + 1) * nk)
        for c in range(tm // LANES):
            lanes = slice(c * LANES, (c + 1) * LANES)
            gate = jnp.zeros((nk, LANES), BF16)
            for h in range(PEER_HEADS):
                na = rows_bf16(na_ref[h, a:a + 1, lanes])
                f1 = rows_bf16(f1_ref[h, a:a + 1, lanes])
                r2 = pltpu.bitcast(r2_ref[h, :, lanes], BF16)
                e2 = pltpu.bitcast(e2_ref[h, :, lanes], BF16)
                gate += jnp.where(r2 < na, e2, zero) * f1
            a_ref[rows, lanes] = gate * jax.nn.gelu(act_ref[rows, lanes]).astype(BF16)
    o_ref[...] += lax.dot_general(a_ref[...], v_ref[...], _TN, preferred_element_type=F32)


def _peer_experts(x_bf, u, v, r2, e2, na, f1, tm=512):
    n, d = x_bf.shape
    ta = na.shape[2]
    te = ta * PEER_NKEYS
    n_exp = u.shape[0]
    tm = min(tm, n)
    assert n % tm == 0 and n_exp == PEER_NKEYS * PEER_NKEYS
    gate_spec = pl.BlockSpec((PEER_HEADS, PEER_NKEYS // 2, tm), lambda i, j: (0, 0, i))
    row_spec = pl.BlockSpec((PEER_HEADS, None, ta, tm), lambda i, j: (0, j, 0, i))
    return pl.pallas_call(
        _peer_expert_kernel,
        grid=(n // tm, n_exp // te),
        in_specs=[pl.BlockSpec((tm, d), lambda i, j: (i, 0)),
                  pl.BlockSpec((te, d), lambda i, j: (j, 0)),
                  pl.BlockSpec((te, d), lambda i, j: (j, 0)),
                  gate_spec, gate_spec, row_spec, row_spec],
        out_specs=pl.BlockSpec((tm, d), lambda i, j: (i, 0)),
        out_shape=jax.ShapeDtypeStruct((n, d), F32),
        scratch_shapes=[pltpu.VMEM((te, tm), F32), pltpu.VMEM((te, tm), BF16)],
        compiler_params=_params("parallel", "arbitrary"),
    )(x_bf, u, v, r2, e2, na, f1)


def _residual_block(x, tok, h3, mk, k_col, mv, v_col, mem_width, want_bf16,
                    w_out, ln1g, ln1b, ln2g, ln2b, wq, keys, u, v):
    rows = x.shape[0]
    mo = _mem_attn(h3, mem_width, mk, k_col, mv, v_col).reshape(rows, mem_width)
    x1, x1_bf = _out_proj_ln(tok, mo, w_out, x, ln1g, ln1b)
    q = _matmul(x1_bf, wq)
    r2, e2, na, f1 = _peer_select(q, keys, PEER_TILE_KEYS)
    y = _peer_experts(x1_bf, u, v, r2, e2, na, f1)
    return _add_ln(x1, y, ln2g, ln2b, want_bf16)


def kernel(x_prompt, x_sample, cache_b_k, cache_b_v, cache_mem_k, cache_mem_v, mem_prompt, w_in_a, sgu_ln_g, sgu_ln_b, sgu_w, sgu_b, w_in_b, rel_bias, w_mem_kv, w_out, ln1_g, ln1_b, ln2_g, ln2_b, peer_wq, peer_keys, peer_u, peer_v):
    bp, s, d = x_prompt.shape
    bs, t, _ = x_sample.shape
    depth = w_out.shape[0]
    assert depth == DEPTH
    n_mem = mem_prompt.shape[1]
    mem_width = w_mem_kv.shape[2] // 2
    tok_width = w_out.shape[1] - mem_width
    mem_hd = mem_width // MEM_HEADS
    b_heads = tok_width // B_HD
    keep = min(B_REACH, s)

    xp = x_prompt.reshape(bp * s, d)
    xs = x_sample.reshape(bs * t, d)
    xp_in, xs_in = xp, xs
    kb_p, vb_p, mk_p_all, mv_p_all, kb_s, vb_s, va_s = [], [], [], [], [], [], []
    for i in range(depth):
        j = i // 2
        kv_p = _matmul(mem_prompt.reshape(bp * n_mem, d), _layer_weight_bf16(w_mem_kv, i))
        kv_p = kv_p.reshape(bp, n_mem, 2 * mem_width)
        mk_p_all.append(kv_p[:, :, :mem_width].reshape(bp, n_mem, MEM_HEADS, mem_hd))
        mv_p_all.append(kv_p[:, :, mem_width:].reshape(bp, n_mem, MEM_HEADS, mem_hd))
        if i % 2 == 0:
            w_in = _layer_weight_bf16(w_in_a, j)
            h_p = _matmul(xp_in, w_in)
            h_s = _matmul(xs_in, w_in)
            tok_p, _ = _sgu(h_p, tok_width, s, sgu_ln_g[j], sgu_ln_b[j], sgu_w[j], sgu_b[j], want_v=False)
            tok_s, v_rows = _sgu(h_s, tok_width, t, sgu_ln_g[j], sgu_ln_b[j], sgu_w[j], sgu_b[j], want_v=True)
            va_s.append(v_rows.reshape(bs, t, tok_width))
            h_p = h_p.reshape(bp, s, -1)
            h_s = h_s.reshape(bs, t, -1)
        else:
            w_in = _layer_weight_bf16(w_in_b, j)
            h_p = _matmul(xp_in, w_in).reshape(bp, s, -1)
            h_s = _matmul(xs_in, w_in).reshape(bs, t, -1)
            tok_p = _band_attention(h_p, None, None, rel_bias[j], tok_width, 2, banded=True)
            tok_p = tok_p.reshape(bp * s, tok_width)
            ck = cache_b_k[j].reshape(bs, -1, tok_width)
            cv = cache_b_v[j].reshape(bs, -1, tok_width)
            tok_s = _band_attention(h_s, ck, cv, rel_bias[j], tok_width, 4, banded=False)
            tok_s = tok_s.reshape(bs * t, tok_width)
            kb_p.append(h_p[:, s - keep:, tok_width:2 * tok_width].reshape(bp, keep, b_heads, B_HD))
            vb_p.append(h_p[:, s - keep:, 2 * tok_width:3 * tok_width].reshape(bp, keep, b_heads, B_HD))
            kb_s.append(h_s[:, :, tok_width:2 * tok_width].reshape(bs, t, b_heads, B_HD))
            vb_s.append(h_s[:, :, 2 * tok_width:3 * tok_width].reshape(bs, t, b_heads, B_HD))
        shared = (_layer_weight_bf16(w_out, i), ln1_g[i], ln1_b[i], ln2_g[i], ln2_b[i],
                  _layer_weight_bf16(peer_wq, i), peer_keys[i].astype(BF16),
                  _layer_weight_bf16(peer_u, i), _layer_weight_bf16(peer_v, i))
        more = i + 1 < depth
        xp, xp_in = _residual_block(xp, tok_p, h_p, kv_p, 0, kv_p, 1, mem_width, more, *shared)
        mk_s = cache_mem_k[i].reshape(bs, n_mem, mem_width)
        mv_s = cache_mem_v[i].reshape(bs, n_mem, mem_width)
        xs, xs_in = _residual_block(xs, tok_s, h_s, mk_s, 0, mv_s, 0, mem_width, more, *shared)
    return (xp.reshape(bp, s, d), xs.reshape(bs, t, d), jnp.stack(kb_p), jnp.stack(vb_p),
            jnp.stack(mk_p_all), jnp.stack(mv_p_all), jnp.stack(kb_s), jnp.stack(vb_s), jnp.stack(va_s))
```

```python
import functools

import jax
import jax.numpy as jnp
from jax import lax
from jax.experimental import pallas as pl
from jax.experimental.pallas import tpu as pltpu

F32 = jnp.float32
BF16 = jnp.bfloat16

DEPTH = 2
CHUNK = 64
MEM_HEADS = 4
SGU_CHUNK = 128
SGU_GROUPS = 8
B_HD = 128
B_BAND_CHUNKS = 8
B_REACH = B_BAND_CHUNKS * CHUNK
REL_CLIP = 128
PEER_HEADS = 8
PEER_NKEYS = 128
PEER_TOPK = 16
PEER_TILE_KEYS = 4
ALPHA = (2 * DEPTH) ** 0.25
LN_EPS = 1e-5
MASK_VALUE = -1e30

VMEM_LIMIT_BYTES = 56 * 1024 * 1024
LANES = 128

_NT = (((1,), (1,)), ((), ()))
_TN = (((0,), (0,)), ((), ()))


def _params(*semantics):
    return pltpu.CompilerParams(dimension_semantics=semantics, vmem_limit_bytes=VMEM_LIMIT_BYTES)


def _layer_norm(x, g, b):
    mu = jnp.mean(x, axis=-1, keepdims=True)
    xc = x - mu
    var = jnp.mean(xc * xc, axis=-1, keepdims=True)
    return xc * lax.rsqrt(var + LN_EPS) * g + b


CAST_BLOCK_BYTES = 8 * 1024 * 1024


def _cast_kernel(w_ref, o_ref):
    o_ref[...] = w_ref[...].astype(o_ref.dtype)


def _layer_weight_bf16(w, layer):
    _, r, c = w.shape
    rows = r
    while rows * c * w.dtype.itemsize > CAST_BLOCK_BYTES and rows % 16 == 0:
        rows //= 2
    return pl.pallas_call(
        _cast_kernel,
        grid=(r // rows,),
        in_specs=[pl.BlockSpec((None, rows, c), lambda i: (layer, i, 0))],
        out_specs=pl.BlockSpec((rows, c), lambda i: (i, 0)),
        out_shape=jax.ShapeDtypeStruct((r, c), BF16),
        compiler_params=_params("parallel"),
    )(w)


def _mm_kernel(a_ref, w_ref, o_ref, abf_ref):
    @pl.when(pl.program_id(1) == 0)
    def _():
        abf_ref[...] = a_ref[...].astype(BF16)

    o_ref[...] = jnp.dot(abf_ref[...], w_ref[...], preferred_element_type=F32).astype(o_ref.dtype)


def _mm_bf16_kernel(a_ref, w_ref, o_ref):
    o_ref[...] = jnp.dot(a_ref[...], w_ref[...], preferred_element_type=F32).astype(o_ref.dtype)


def _matmul(a, w, out_dtype=F32):
    m, k = a.shape
    n = w.shape[1]
    cast = a.dtype != BF16
    tm = min(512 if cast else 1024, m)
    tn = next(c for c in (1024, 512, 256, LANES) if n % c == 0)
    assert m % tm == 0
    return pl.pallas_call(
        _mm_kernel if cast else _mm_bf16_kernel,
        grid=(m // tm, n // tn),
        in_specs=[pl.BlockSpec((tm, k), lambda i, j: (i, 0)),
                  pl.BlockSpec((k, tn), lambda i, j: (0, j))],
        out_specs=pl.BlockSpec((tm, tn), lambda i, j: (i, j)),
        out_shape=jax.ShapeDtypeStruct((m, n), out_dtype),
        scratch_shapes=[pltpu.VMEM((tm, k), BF16)] if cast else [],
        compiler_params=_params("parallel", "arbitrary"),
    )(a, w)


def _out_ln_kernel(tok_ref, mo_ref, w1_ref, w2_ref, x_ref, g_ref, b_ref, o_ref, obf_ref, *, tn, nj):
    j = pl.program_id(1)
    acc = jnp.dot(tok_ref[...], w1_ref[...], preferred_element_type=F32)
    acc += jnp.dot(mo_ref[...], w2_ref[...], preferred_element_type=F32)
    col = pl.multiple_of(j * tn, tn)
    o_ref[:, pl.ds(col, tn)] = ALPHA * x_ref[...] + acc

    @pl.when(j == nj - 1)
    def _():
        y = _layer_norm(o_ref[...], g_ref[...], b_ref[...])
        o_ref[...] = y
        obf_ref[...] = y.astype(BF16)


def _out_proj_ln(tok, mo, w_out, x, g, b, tm=512, tn=512):
    m, d = x.shape
    kt, km = tok.shape[1], mo.shape[1]
    assert kt % km == 0 and w_out.shape == (kt + km, d)
    tm, tn = min(tm, m), min(tn, d)
    assert m % tm == 0 and d % tn == 0
    nj = d // tn
    return pl.pallas_call(
        functools.partial(_out_ln_kernel, tn=tn, nj=nj),
        grid=(m // tm, nj),
        in_specs=[pl.BlockSpec((tm, kt), lambda i, j: (i, 0)),
                  pl.BlockSpec((tm, km), lambda i, j: (i, 0)),
                  pl.BlockSpec((kt, tn), lambda i, j: (0, j)),
                  pl.BlockSpec((km, tn), lambda i, j: (kt // km, j)),
                  pl.BlockSpec((tm, tn), lambda i, j: (i, j)),
                  pl.BlockSpec((1, d), lambda i, j: (0, 0)),
                  pl.BlockSpec((1, d), lambda i, j: (0, 0))],
        out_specs=[pl.BlockSpec((tm, d), lambda i, j: (i, 0)),
                   pl.BlockSpec((tm, d), lambda i, j: (i, 0))],
        out_shape=[jax.ShapeDtypeStruct((m, d), F32), jax.ShapeDtypeStruct((m, d), BF16)],
        compiler_params=_params("parallel", "arbitrary"),
    )(tok, mo, w_out, w_out, x, g.reshape(1, d), b.reshape(1, d))


def _sgu_kernel(u_ref, v_ref, g_ref, b_ref, w_ref, bias_ref, tok_ref, *vn_refs, groups):
    n = u_ref.shape[0]
    gw = u_ref.shape[1] // groups
    vn = _layer_norm(jax.nn.gelu(v_ref[...]), g_ref[...], b_ref[...])
    if vn_refs:
        vn_refs[0][...] = vn
    causal = lax.broadcasted_iota(jnp.int32, (n, n), 1) <= lax.broadcasted_iota(jnp.int32, (n, n), 0)
    for g in range(groups):
        cols = slice(g * gw, (g + 1) * gw)
        w = jnp.where(causal, w_ref[g], 0.0).astype(BF16)
        s = jnp.dot(w, vn[:, cols].astype(BF16), preferred_element_type=F32) + bias_ref[:, cols]
        tok_ref[:, cols] = (jax.nn.gelu(u_ref[:, cols]) * s).astype(tok_ref.dtype)


def _sgu(h, tok_width, t, ln_g, ln_b, w_s, b_s, want_v):
    rows = h.shape[0]
    n = min(t, SGU_CHUNK)
    assert t % n == 0 and tok_width % SGU_GROUPS == 0
    gw = tok_width // SGU_GROUPS
    w = w_s[:, :n, :n]
    bias = jnp.repeat(b_s[:, :n].T, gw, axis=1)
    out_shape = [jax.ShapeDtypeStruct((rows, tok_width), BF16)]
    out_specs = [pl.BlockSpec((n, tok_width), lambda c: (c, 0))]
    if want_v:
        out_shape.append(jax.ShapeDtypeStruct((rows, tok_width), F32))
        out_specs.append(pl.BlockSpec((n, tok_width), lambda c: (c, 0)))
    vec = pl.BlockSpec((1, tok_width), lambda c: (0, 0))
    res = pl.pallas_call(
        functools.partial(_sgu_kernel, groups=SGU_GROUPS),
        grid=(rows // n,),
        in_specs=[pl.BlockSpec((n, tok_width), lambda c: (c, 0)),
                  pl.BlockSpec((n, tok_width), lambda c: (c, 1)),
                  vec, vec,
                  pl.BlockSpec((SGU_GROUPS, n, n), lambda c: (0, 0, 0)),
                  pl.BlockSpec((n, tok_width), lambda c: (0, 0))],
        out_specs=out_specs,
        out_shape=out_shape,
        compiler_params=_params("parallel"),
    )(h, h, ln_g.reshape(1, -1), ln_b.reshape(1, -1), w, bias)
    return res if want_v else (res[0], None)


def _mem_attn_kernel(q_ref, mk_ref, mv_ref, o_ref, *, head_rows):
    hd = q_ref.shape[1] // MEM_HEADS
    pieces = hd // LANES
    for h in range(MEM_HEADS):
        def piece(ref, c):
            if head_rows:
                period = MEM_HEADS * pieces
                return ref[pl.ds(h * pieces + c, ref.shape[0] // period, stride=period), :].astype(BF16)
            return ref[:, h * hd + c * LANES:h * hd + (c + 1) * LANES].astype(BF16)

        s = 0.0
        for c in range(pieces):
            q = q_ref[:, h * hd + c * LANES:h * hd + (c + 1) * LANES].astype(BF16)
            s += lax.dot_general(q, piece(mk_ref, c), _NT, preferred_element_type=F32)
        s *= hd ** -0.5
        p = jnp.exp(s - jnp.max(s, axis=-1, keepdims=True))
        l = jnp.sum(p, axis=-1, keepdims=True)
        p = p.astype(BF16)
        for c in range(pieces):
            o = jnp.dot(p, piece(mv_ref, c), preferred_element_type=F32)
            o_ref[:, h * hd + c * LANES:h * hd + (c + 1) * LANES] = (o / l).astype(o_ref.dtype)


def _mem_attn(h3, mem_width, mk, k_col, mv, v_col, head_rows, tq=512):
    bsz, t, w = h3.shape
    assert w % mem_width == 0
    tq = min(tq, t)
    assert t % tq == 0
    q_col = w // mem_width - 1
    kv_block = mk.shape[1:] if head_rows else (mk.shape[1], mem_width)
    return pl.pallas_call(
        functools.partial(_mem_attn_kernel, head_rows=head_rows),
        grid=(bsz, t // tq),
        in_specs=[pl.BlockSpec((None, tq, mem_width), lambda b, i: (b, i, q_col)),
                  pl.BlockSpec((None,) + kv_block, lambda b, i: (b, 0, k_col)),
                  pl.BlockSpec((None,) + kv_block, lambda b, i: (b, 0, v_col))],
        out_specs=pl.BlockSpec((None, tq, mem_width), lambda b, i: (b, i, 0)),
        out_shape=jax.ShapeDtypeStruct((bsz, t, mem_width), BF16),
        compiler_params=_params("parallel", "parallel"),
    )(h3, mk, mv)


def _band_kernel(q_ref, kp_ref, kc_ref, vp_ref, vc_ref, t_ref, o_ref, bias_ref, *, heads, banded, head_rows):
    tq = q_ref.shape[0]
    n_keys = bias_ref.shape[2]
    n_prev = n_keys - tq

    @pl.when((pl.program_id(1) == 0) & (pl.program_id(2) == 0))
    def _():
        row = lax.broadcasted_iota(jnp.int32, (tq, n_keys), 0)
        col = lax.broadcasted_iota(jnp.int32, (tq, n_keys), 1)
        ahead = col // CHUNK - row // CHUNK
        for h in range(heads):
            table = jnp.broadcast_to(t_ref[h], (tq, t_ref.shape[2]))
            bias = pltpu.roll(table, 0, 1, stride=1, stride_axis=0)[:, n_prev:n_prev + n_keys]
            if banded:
                bias = jnp.where(ahead >= 0, jnp.where(ahead <= B_BAND_CHUNKS, bias, MASK_VALUE), MASK_VALUE)
            bias_ref[h] = bias

    scale = B_HD ** -0.5
    for h in range(heads):
        cols = slice(h * B_HD, (h + 1) * B_HD)
        q = q_ref[:, cols].astype(BF16)
        if head_rows:
            kp, vp = (r[pl.ds(h, n_prev, stride=heads), :] for r in (kp_ref, vp_ref))
        else:
            kp, vp = kp_ref[:, cols], vp_ref[:, cols]
        s0 = lax.dot_general(q, kp.astype(BF16), _NT, preferred_element_type=F32) * scale
        s1 = lax.dot_general(q, kc_ref[:, cols].astype(BF16), _NT, preferred_element_type=F32) * scale
        s0 += bias_ref[h, :, :n_prev]
        s1 += bias_ref[h, :, n_prev:]
        if banded:
            s0 = jnp.where(pl.program_id(2) == 0, MASK_VALUE, s0)
        m = jnp.maximum(jnp.max(s0, axis=-1, keepdims=True), jnp.max(s1, axis=-1, keepdims=True))
        p0 = jnp.exp(s0 - m)
        p1 = jnp.exp(s1 - m)
        l = jnp.sum(p0, axis=-1, keepdims=True) + jnp.sum(p1, axis=-1, keepdims=True)
        o = jnp.dot(p0.astype(BF16), vp.astype(BF16), preferred_element_type=F32)
        o += jnp.dot(p1.astype(BF16), vc_ref[:, cols].astype(BF16), preferred_element_type=F32)
        o_ref[:, cols] = (o / l).astype(o_ref.dtype)


def _rel_bias_table(rel_bias, n_prev, length):
    lo = 2 * n_prev - REL_CLIP
    hi = lo + 2 * REL_CLIP + 1
    assert 0 <= lo and hi <= length
    heads = rel_bias.shape[0]
    table = jnp.concatenate([jnp.broadcast_to(rel_bias[:, -1:], (heads, lo)),
                             rel_bias[:, ::-1],
                             jnp.broadcast_to(rel_bias[:, :1], (heads, length - hi))], axis=1)
    return table[:, None, :]


def _band_attention(q3, prev_k, prev_v, rel_bias, tok_width, heads_per_step, banded):
    bsz, t, _ = q3.shape
    tq = min(t, B_REACH)
    hw = heads_per_step * B_HD
    ng = tok_width // hw
    new = lambda idx: pl.BlockSpec((None, tq, hw), idx)
    if prev_k is None:
        n_prev = B_REACH
        before = lambda i: jnp.maximum(i - 1, 0)
        prev_k = prev_v = q3
        k_spec = pl.BlockSpec((None, n_prev, hw), lambda g, b, i: (b, before(i), ng + g))
        v_spec = pl.BlockSpec((None, n_prev, hw), lambda g, b, i: (b, before(i), 2 * ng + g))
    else:
        assert ng == 1 and prev_k.shape[2] == B_HD
        n_prev = prev_k.shape[1] // heads_per_step
        k_spec = v_spec = pl.BlockSpec((None,) + prev_k.shape[1:], lambda g, b, i: (b, 0, 0))
    assert t % tq == 0 and tok_width % hw == 0 and n_prev % LANES == 0 and tq <= n_prev
    n_keys = n_prev + tq
    length = pl.next_power_of_2(n_prev + n_keys)
    table = _rel_bias_table(rel_bias, n_prev, length)
    return pl.pallas_call(
        functools.partial(_band_kernel, heads=heads_per_step, banded=banded, head_rows=prev_k is not q3),
        grid=(ng, bsz, t // tq),
        in_specs=[new(lambda g, b, i: (b, i, g)),
                  k_spec,
                  new(lambda g, b, i: (b, i, ng + g)),
                  v_spec,
                  new(lambda g, b, i: (b, i, 2 * ng + g)),
                  pl.BlockSpec((heads_per_step, 1, length), lambda g, b, i: (g, 0, 0))],
        out_specs=new(lambda g, b, i: (b, i, g)),
        out_shape=jax.ShapeDtypeStruct((bsz, t, tok_width), BF16),
        scratch_shapes=[pltpu.VMEM((heads_per_step, tq, n_keys), F32)],
        compiler_params=_params("arbitrary", "arbitrary", "arbitrary"),
    )(q3, prev_k, q3, prev_v, q3, table)


def _top16_rows(vals, order):
    rank = jnp.full(vals.shape, float(PEER_TOPK), F32)
    big = jnp.int32(2 ** 30)
    tops = []
    for r in range(PEER_TOPK):
        m = jnp.max(vals, axis=0, keepdims=True)
        first = jnp.min(jnp.where(vals == m, order, big), axis=0, keepdims=True)
        hit = order == first
        rank = jnp.where(hit, float(r), rank)
        vals = jnp.where(hit, -jnp.inf, vals)
        tops.append(m)
    return tops, rank


def _peer_select_kernel(q_ref, keys_ref, r2_ref, e2_ref, na_ref, f1_ref):
    nk, k = PEER_NKEYS, PEER_TOPK
    tq = q_ref.shape[0]
    row = lax.broadcasted_iota(jnp.int32, (nk, tq), 0)
    jrow = lax.broadcasted_iota(jnp.int32, (k, tq), 0)
    n_full = k // 2
    crow = lax.broadcasted_iota(jnp.int32, (n_full * k + k - n_full, tq), 0)
    flat = jnp.where(crow < n_full * k, crow, (crow - n_full * k + n_full) * k)
    for h in range(PEER_HEADS):
        tops, ranks, exps = [], [], []
        for p in range(2):
            c0 = (h * 2 + p) * nk
            qc = q_ref[:, c0:c0 + nk].astype(BF16)
            s = lax.dot_general(keys_ref[h, p], qc, _NT, preferred_element_type=F32)
            t, r = _top16_rows(s, row)
            tops.append(jnp.concatenate(t, axis=0))
            ranks.append(r)
            exps.append(jnp.exp(s - t[0]))
        t1, t2 = tops
        e1r = jnp.exp(t1 - t1[0:1])
        e2r = jnp.exp(t2 - t2[0:1])
        cand, wgt = [], []
        for i in range(n_full):
            c = t1[i:i + 1] + t2
            cand.append(jnp.where(jrow < k // (i + 1), c, -jnp.inf))
            wgt.append(e1r[i:i + 1] * e2r)
        cand.append(t1[n_full:] + t2[0:1])
        wgt.append(e1r[n_full:] * e2r[0:1])
        cand = jnp.concatenate(cand, axis=0)
        wgt = jnp.concatenate(wgt, axis=0)
        _, crank = _top16_rows(cand, flat)
        sel = jnp.where(crank < float(k), 1.0, 0.0)
        z = jnp.sum(sel * wgt, axis=0, keepdims=True)
        na = jnp.zeros((nk, tq), F32)
        for i in range(k):
            if i < n_full:
                n_i = jnp.sum(sel[i * k:(i + 1) * k], axis=0, keepdims=True)
            else:
                n_i = sel[n_full * k + i - n_full:n_full * k + i - n_full + 1]
            na = jnp.where(ranks[0] == float(i), n_i, na)
        r2_ref[h] = pltpu.bitcast(ranks[1].astype(BF16), jnp.uint32)
        e2_ref[h] = pltpu.bitcast(exps[1].astype(BF16), jnp.uint32)
        f1 = exps[0] / z
        ta = na_ref.shape[2]
        for g in range(nk // ta):
            na_ref[h, g] = na[g * ta:(g + 1) * ta]
            f1_ref[h, g] = f1[g * ta:(g + 1) * ta]


def _peer_select(q, keys, ta, tq=128):
    n = q.shape[0]
    assert n % tq == 0 and keys.shape == (PEER_HEADS, 2, PEER_NKEYS, q.shape[1] // (2 * PEER_HEADS))
    assert keys.shape[3] == PEER_NKEYS
    out = jax.ShapeDtypeStruct((PEER_HEADS, PEER_NKEYS // 2, n), jnp.uint32)
    spec = pl.BlockSpec((PEER_HEADS, PEER_NKEYS // 2, tq), lambda i: (0, 0, i))
    out_g = jax.ShapeDtypeStruct((PEER_HEADS, PEER_NKEYS // ta, ta, n), F32)
    spec_g = pl.BlockSpec((PEER_HEADS, PEER_NKEYS // ta, ta, tq), lambda i: (0, 0, 0, i))
    return pl.pallas_call(
        _peer_select_kernel,
        grid=(n // tq,),
        in_specs=[pl.BlockSpec((tq, q.shape[1]), lambda i: (i, 0)),
                  pl.BlockSpec(keys.shape, lambda i: (0, 0, 0, 0))],
        out_specs=[spec, spec, spec_g, spec_g],
        out_shape=[out, out, out_g, out_g],
        compiler_params=_params("parallel"),
    )(q, keys)


def _peer_expert_kernel(x_ref, u_ref, v_ref, r2_ref, e2_ref, na_ref, f1_ref, res_hbm, g_ref, b_ref,
                        o_ref, act_ref, a_ref, res_sem):
    i, j = pl.program_id(0), pl.program_id(1)
    te, tm = act_ref.shape
    nk = PEER_NKEYS

    def residual_copy():
        rows = pl.ds(pl.multiple_of(i * tm, tm), tm)
        return pltpu.make_async_copy(res_hbm.at[rows, :], o_ref, res_sem)

    @pl.when(j == 0)
    def _():
        residual_copy().start()

    act_ref[...] = lax.dot_general(u_ref[...], x_ref[...], _NT, preferred_element_type=F32)
    zero = jnp.zeros((), BF16)

    def rows_bf16(row):
        bits = pltpu.bitcast(row.astype(BF16).astype(F32), jnp.uint32) & jnp.uint32(0xFFFF0000)
        return pltpu.bitcast(jnp.broadcast_to(bits | (bits >> 16), (nk // 2, LANES)), BF16)

    for a in range(te // nk):
        rows = slice(a * nk, (a + 1) * nk)
        for c in range(tm // LANES):
            lanes = slice(c * LANES, (c + 1) * LANES)
            gate = jnp.zeros((nk, LANES), BF16)
            for h in range(PEER_HEADS):
                na = rows_bf16(na_ref[h, a:a + 1, lanes])
                f1 = rows_bf16(f1_ref[h, a:a + 1, lanes])
                r2 = pltpu.bitcast(r2_ref[h, :, lanes], BF16)
                e2 = pltpu.bitcast(e2_ref[h, :, lanes], BF16)
                gate += jnp.where(r2 < na, e2, zero) * f1
            a_ref[rows, lanes] = gate * jax.nn.gelu(act_ref[rows, lanes]).astype(BF16)

    @pl.when(j == 0)
    def _():
        residual_copy().wait()
        o_ref[...] = ALPHA * o_ref[...]

    o_ref[...] += lax.dot_general(a_ref[...], v_ref[...], _TN, preferred_element_type=F32)

    @pl.when(j == pl.num_programs(1) - 1)
    def _():
        o_ref[...] = _layer_norm(o_ref[...], g_ref[...], b_ref[...])


def _peer_experts(x_bf, u, v, r2, e2, na, f1, residual, g, b, tm=512):
    n, d = x_bf.shape
    ta = na.shape[2]
    te = ta * PEER_NKEYS
    n_exp = u.shape[0]
    tm = min(tm, n)
    assert n % tm == 0 and n_exp == PEER_NKEYS * PEER_NKEYS
    gate_spec = pl.BlockSpec((PEER_HEADS, PEER_NKEYS // 2, tm), lambda i, j: (0, 0, i))
    row_spec = pl.BlockSpec((PEER_HEADS, None, ta, tm), lambda i, j: (0, j, 0, i))
    return pl.pallas_call(
        _peer_expert_kernel,
        grid=(n // tm, n_exp // te),
        in_specs=[pl.BlockSpec((tm, d), lambda i, j: (i, 0)),
                  pl.BlockSpec((te, d), lambda i, j: (j, 0)),
                  pl.BlockSpec((te, d), lambda i, j: (j, 0)),
                  gate_spec, gate_spec, row_spec, row_spec,
                  pl.BlockSpec(memory_space=pl.ANY),
                  pl.BlockSpec((1, d), lambda i, j: (0, 0)),
                  pl.BlockSpec((1, d), lambda i, j: (0, 0))],
        out_specs=pl.BlockSpec((tm, d), lambda i, j: (i, 0)),
        out_shape=jax.ShapeDtypeStruct((n, d), F32),
        scratch_shapes=[pltpu.VMEM((te, tm), F32), pltpu.VMEM((te, tm), BF16), pltpu.SemaphoreType.DMA(())],
        compiler_params=_params("arbitrary", "arbitrary"),
    )(x_bf, u, v, r2, e2, na, f1, residual, g.reshape(1, d), b.reshape(1, d))


def _residual_block(x, tok, h3, mk, k_col, mv, v_col, head_rows, mem_width,
                    w_out, ln1g, ln1b, ln2g, ln2b, wq, keys, u, v):
    rows = x.shape[0]
    mo = _mem_attn(h3, mem_width, mk, k_col, mv, v_col, head_rows).reshape(rows, mem_width)
    x1, x1_bf = _out_proj_ln(tok, mo, w_out, x, ln1g, ln1b)
    q = _matmul(x1_bf, wq)
    r2, e2, na, f1 = _peer_select(q, keys, PEER_TILE_KEYS)
    return _peer_experts(x1_bf, u, v, r2, e2, na, f1, x1, ln2g, ln2b)


def kernel(x_prompt, x_sample, cache_b_k, cache_b_v, cache_mem_k, cache_mem_v, mem_prompt, w_in_a, sgu_ln_g, sgu_ln_b, sgu_w, sgu_b, w_in_b, rel_bias, w_mem_kv, w_out, ln1_g, ln1_b, ln2_g, ln2_b, peer_wq, peer_keys, peer_u, peer_v):
    bp, s, d = x_prompt.shape
    bs, t, _ = x_sample.shape
    depth = w_out.shape[0]
    assert depth == DEPTH
    n_mem = mem_prompt.shape[1]
    mem_width = w_mem_kv.shape[2] // 2
    tok_width = w_out.shape[1] - mem_width
    mem_hd = mem_width // MEM_HEADS
    b_heads = tok_width // B_HD
    keep = min(B_REACH, s)

    xp = x_prompt.reshape(bp * s, d)
    xs = x_sample.reshape(bs * t, d)
    kb_p, vb_p, mk_p_all, mv_p_all, kb_s, vb_s, va_s = [], [], [], [], [], [], []
    for i in range(depth):
        j = i // 2
        kv_p = _matmul(mem_prompt.reshape(bp * n_mem, d), _layer_weight_bf16(w_mem_kv, i))
        kv_p = kv_p.reshape(bp, n_mem, 2 * mem_width)
        mk_p_all.append(kv_p[:, :, :mem_width].reshape(bp, n_mem, MEM_HEADS, mem_hd))
        mv_p_all.append(kv_p[:, :, mem_width:].reshape(bp, n_mem, MEM_HEADS, mem_hd))
        if i % 2 == 0:
            w_in = _layer_weight_bf16(w_in_a, j)
            h_p = _matmul(xp, w_in)
            h_s = _matmul(xs, w_in)
            tok_p, _ = _sgu(h_p, tok_width, s, sgu_ln_g[j], sgu_ln_b[j], sgu_w[j], sgu_b[j], want_v=False)
            tok_s, v_rows = _sgu(h_s, tok_width, t, sgu_ln_g[j], sgu_ln_b[j], sgu_w[j], sgu_b[j], want_v=True)
            va_s.append(v_rows.reshape(bs, t, tok_width))
            h_p = h_p.reshape(bp, s, -1)
            h_s = h_s.reshape(bs, t, -1)
        else:
            w_in = _layer_weight_bf16(w_in_b, j)
            h_p = _matmul(xp, w_in).reshape(bp, s, -1)
            h_s = _matmul(xs, w_in).reshape(bs, t, -1)
            tok_p = _band_attention(h_p, None, None, rel_bias[j], tok_width, 2, banded=True)
            tok_p = tok_p.reshape(bp * s, tok_width)
            ck = cache_b_k[j].reshape(bs, -1, B_HD)
            cv = cache_b_v[j].reshape(bs, -1, B_HD)
            tok_s = _band_attention(h_s, ck, cv, rel_bias[j], tok_width, b_heads, banded=False)
            tok_s = tok_s.reshape(bs * t, tok_width)
            kb_p.append(h_p[:, s - keep:, tok_width:2 * tok_width].reshape(bp, keep, b_heads, B_HD))
            vb_p.append(h_p[:, s - keep:, 2 * tok_width:3 * tok_width].reshape(bp, keep, b_heads, B_HD))
            kb_s.append(h_s[:, :, tok_width:2 * tok_width].reshape(bs, t, b_heads, B_HD))
            vb_s.append(h_s[:, :, 2 * tok_width:3 * tok_width].reshape(bs, t, b_heads, B_HD))
        shared = (_layer_weight_bf16(w_out, i), ln1_g[i], ln1_b[i], ln2_g[i], ln2_b[i],
                  _layer_weight_bf16(peer_wq, i), peer_keys[i].astype(BF16),
                  _layer_weight_bf16(peer_u, i), _layer_weight_bf16(peer_v, i))
        xp = _residual_block(xp, tok_p, h_p, kv_p, 0, kv_p, 1, False, mem_width, *shared)
        mk_s = cache_mem_k[i].reshape(bs, -1, LANES)
        mv_s = cache_mem_v[i].reshape(bs, -1, LANES)
        xs = _residual_block(xs, tok_s, h_s, mk_s, 0, mv_s, 0, True, mem_width, *shared)
    return (xp.reshape(bp, s, d), xs.reshape(bs, t, d), jnp.stack(kb_p), jnp.stack(vb_p),
            jnp.stack(mk_p_all), jnp.stack(mv_p_all), jnp.stack(kb_s), jnp.stack(vb_s), jnp.stack(va_s))
```

```python
import functools

import jax
import jax.numpy as jnp
from jax import lax
from jax.experimental import pallas as pl
from jax.experimental.pallas import tpu as pltpu

F32 = jnp.float32
BF16 = jnp.bfloat16

DEPTH = 2
CHUNK = 64
MEM_HEADS = 4
SGU_CHUNK = 128
SGU_GROUPS = 8
B_HD = 128
B_BAND_CHUNKS = 8
B_REACH = B_BAND_CHUNKS * CHUNK
REL_CLIP = 128
PEER_HEADS = 8
PEER_NKEYS = 128
PEER_TOPK = 16
PEER_TILE_KEYS = 4
ALPHA = (2 * DEPTH) ** 0.25
LN_EPS = 1e-5
MASK_VALUE = -1e30

VMEM_LIMIT_BYTES = 56 * 1024 * 1024
LANES = 128

_NT = (((1,), (1,)), ((), ()))
_TN = (((0,), (0,)), ((), ()))


def _params(*semantics):
    return pltpu.CompilerParams(dimension_semantics=semantics, vmem_limit_bytes=VMEM_LIMIT_BYTES)


def _layer_norm(x, g, b):
    mu = jnp.mean(x, axis=-1, keepdims=True)
    xc = x - mu
    var = jnp.mean(xc * xc, axis=-1, keepdims=True)
    return xc * lax.rsqrt(var + LN_EPS) * g + b


CAST_BLOCK_BYTES = 8 * 1024 * 1024


def _cast_kernel(w_ref, o_ref):
    o_ref[...] = w_ref[...].astype(o_ref.dtype)


def _layer_weight_bf16(w, layer):
    _, r, c = w.shape
    rows = r
    while rows * c * w.dtype.itemsize > CAST_BLOCK_BYTES and rows % 16 == 0:
        rows //= 2
    return pl.pallas_call(
        _cast_kernel,
        grid=(r // rows,),
        in_specs=[pl.BlockSpec((None, rows, c), lambda i: (layer, i, 0))],
        out_specs=pl.BlockSpec((rows, c), lambda i: (i, 0)),
        out_shape=jax.ShapeDtypeStruct((r, c), BF16),
        compiler_params=_params("parallel"),
    )(w)


def _mm_kernel(a_ref, w_ref, o_ref, abf_ref):
    @pl.when(pl.program_id(1) == 0)
    def _():
        abf_ref[...] = a_ref[...].astype(BF16)

    o_ref[...] = jnp.dot(abf_ref[...], w_ref[...], preferred_element_type=F32).astype(o_ref.dtype)


def _mm_bf16_kernel(a_ref, w_ref, o_ref):
    o_ref[...] = jnp.dot(a_ref[...], w_ref[...], preferred_element_type=F32).astype(o_ref.dtype)


def _matmul(a, w, out_dtype=F32):
    m, k = a.shape
    n = w.shape[1]
    cast = a.dtype != BF16
    tm = min(512 if cast else 1024, m)
    tn = next(c for c in (1024, 512, 256, LANES) if n % c == 0)
    assert m % tm == 0
    return pl.pallas_call(
        _mm_kernel if cast else _mm_bf16_kernel,
        grid=(m // tm, n // tn),
        in_specs=[pl.BlockSpec((tm, k), lambda i, j: (i, 0)),
                  pl.BlockSpec((k, tn), lambda i, j: (0, j))],
        out_specs=pl.BlockSpec((tm, tn), lambda i, j: (i, j)),
        out_shape=jax.ShapeDtypeStruct((m, n), out_dtype),
        scratch_shapes=[pltpu.VMEM((tm, k), BF16)] if cast else [],
        compiler_params=_params("parallel", "arbitrary"),
    )(a, w)


def _out_ln_kernel(tok_ref, mo_ref, w1_ref, w2_ref, x_ref, g_ref, b_ref, o_ref, obf_ref, *, tn, nj):
    j = pl.program_id(1)
    acc = jnp.dot(tok_ref[...], w1_ref[...], preferred_element_type=F32)
    acc += jnp.dot(mo_ref[...], w2_ref[...], preferred_element_type=F32)
    col = pl.multiple_of(j * tn, tn)
    o_ref[:, pl.ds(col, tn)] = ALPHA * x_ref[...] + acc

    @pl.when(j == nj - 1)
    def _():
        y = _layer_norm(o_ref[...], g_ref[...], b_ref[...])
        o_ref[...] = y
        obf_ref[...] = y.astype(BF16)


def _out_proj_ln(tok, mo, w_out, x, g, b, tm=512, tn=512):
    m, d = x.shape
    kt, km = tok.shape[1], mo.shape[1]
    assert kt % km == 0 and w_out.shape == (kt + km, d)
    tm, tn = min(tm, m), min(tn, d)
    assert m % tm == 0 and d % tn == 0
    nj = d // tn
    return pl.pallas_call(
        functools.partial(_out_ln_kernel, tn=tn, nj=nj),
        grid=(m // tm, nj),
        in_specs=[pl.BlockSpec((tm, kt), lambda i, j: (i, 0)),
                  pl.BlockSpec((tm, km), lambda i, j: (i, 0)),
                  pl.BlockSpec((kt, tn), lambda i, j: (0, j)),
                  pl.BlockSpec((km, tn), lambda i, j: (kt // km, j)),
                  pl.BlockSpec((tm, tn), lambda i, j: (i, j)),
                  pl.BlockSpec((1, d), lambda i, j: (0, 0)),
                  pl.BlockSpec((1, d), lambda i, j: (0, 0))],
        out_specs=[pl.BlockSpec((tm, d), lambda i, j: (i, 0)),
                   pl.BlockSpec((tm, d), lambda i, j: (i, 0))],
        out_shape=[jax.ShapeDtypeStruct((m, d), F32), jax.ShapeDtypeStruct((m, d), BF16)],
        compiler_params=_params("parallel", "arbitrary"),
    )(tok, mo, w_out, w_out, x, g.reshape(1, d), b.reshape(1, d))


def _sgu_kernel(u_ref, v_ref, g_ref, b_ref, w_ref, bias_ref, tok_ref, *vn_refs, groups):
    n = u_ref.shape[0]
    gw = u_ref.shape[1] // groups
    vn = _layer_norm(jax.nn.gelu(v_ref[...]), g_ref[...], b_ref[...])
    if vn_refs:
        vn_refs[0][...] = vn
    causal = lax.broadcasted_iota(jnp.int32, (n, n), 1) <= lax.broadcasted_iota(jnp.int32, (n, n), 0)
    for g in range(groups):
        cols = slice(g * gw, (g + 1) * gw)
        w = jnp.where(causal, w_ref[g], 0.0).astype(BF16)
        s = jnp.dot(w, vn[:, cols].astype(BF16), preferred_element_type=F32) + bias_ref[:, cols]
        tok_ref[:, cols] = (jax.nn.gelu(u_ref[:, cols]) * s).astype(tok_ref.dtype)


def _sgu(h, tok_width, t, ln_g, ln_b, w_s, b_s, want_v):
    rows = h.shape[0]
    n = min(t, SGU_CHUNK)
    assert t % n == 0 and tok_width % SGU_GROUPS == 0
    gw = tok_width // SGU_GROUPS
    w = w_s[:, :n, :n]
    bias = jnp.repeat(b_s[:, :n].T, gw, axis=1)
    out_shape = [jax.ShapeDtypeStruct((rows, tok_width), BF16)]
    out_specs = [pl.BlockSpec((n, tok_width), lambda c: (c, 0))]
    if want_v:
        out_shape.append(jax.ShapeDtypeStruct((rows, tok_width), F32))
        out_specs.append(pl.BlockSpec((n, tok_width), lambda c: (c, 0)))
    vec = pl.BlockSpec((1, tok_width), lambda c: (0, 0))
    res = pl.pallas_call(
        functools.partial(_sgu_kernel, groups=SGU_GROUPS),
        grid=(rows // n,),
        in_specs=[pl.BlockSpec((n, tok_width), lambda c: (c, 0)),
                  pl.BlockSpec((n, tok_width), lambda c: (c, 1)),
                  vec, vec,
                  pl.BlockSpec((SGU_GROUPS, n, n), lambda c: (0, 0, 0)),
                  pl.BlockSpec((n, tok_width), lambda c: (0, 0))],
        out_specs=out_specs,
        out_shape=out_shape,
        compiler_params=_params("parallel"),
    )(h, h, ln_g.reshape(1, -1), ln_b.reshape(1, -1), w, bias)
    return res if want_v else (res[0], None)


def _mem_attn_kernel(q_ref, mk_ref, mv_ref, o_ref, *, head_rows):
    hd = q_ref.shape[1] // MEM_HEADS
    pieces = hd // LANES
    for h in range(MEM_HEADS):
        def piece(ref, c):
            if head_rows:
                period = MEM_HEADS * pieces
                return ref[pl.ds(h * pieces + c, ref.shape[0] // period, stride=period), :].astype(BF16)
            return ref[:, h * hd + c * LANES:h * hd + (c + 1) * LANES].astype(BF16)

        s = 0.0
        for c in range(pieces):
            q = q_ref[:, h * hd + c * LANES:h * hd + (c + 1) * LANES].astype(BF16)
            s += lax.dot_general(q, piece(mk_ref, c), _NT, preferred_element_type=F32)
        s *= hd ** -0.5
        p = jnp.exp(s - jnp.max(s, axis=-1, keepdims=True))
        l = jnp.sum(p, axis=-1, keepdims=True)
        p = p.astype(BF16)
        for c in range(pieces):
            o = jnp.dot(p, piece(mv_ref, c), preferred_element_type=F32)
            o_ref[:, h * hd + c * LANES:h * hd + (c + 1) * LANES] = (o / l).astype(o_ref.dtype)


def _mem_attn(h3, mem_width, mk, k_col, mv, v_col, head_rows, tq=512):
    bsz, t, w = h3.shape
    assert w % mem_width == 0
    tq = min(tq, t)
    assert t % tq == 0
    q_col = w // mem_width - 1
    kv_block = mk.shape[1:] if head_rows else (mk.shape[1], mem_width)
    return pl.pallas_call(
        functools.partial(_mem_attn_kernel, head_rows=head_rows),
        grid=(bsz, t // tq),
        in_specs=[pl.BlockSpec((None, tq, mem_width), lambda b, i: (b, i, q_col)),
                  pl.BlockSpec((None,) + kv_block, lambda b, i: (b, 0, k_col)),
                  pl.BlockSpec((None,) + kv_block, lambda b, i: (b, 0, v_col))],
        out_specs=pl.BlockSpec((None, tq, mem_width), lambda b, i: (b, i, 0)),
        out_shape=jax.ShapeDtypeStruct((bsz, t, mem_width), BF16),
        compiler_params=_params("parallel", "parallel"),
    )(h3, mk, mv)


def _band_kernel(q_ref, kp_ref, kc_ref, vp_ref, vc_ref, t_ref, o_ref, bias_ref, *, heads, banded, head_rows):
    tq = q_ref.shape[0]
    n_keys = bias_ref.shape[2]
    n_prev = n_keys - tq

    @pl.when((pl.program_id(1) == 0) & (pl.program_id(2) == 0))
    def _():
        row = lax.broadcasted_iota(jnp.int32, (tq, n_keys), 0)
        col = lax.broadcasted_iota(jnp.int32, (tq, n_keys), 1)
        ahead = col // CHUNK - row // CHUNK
        for h in range(heads):
            table = jnp.broadcast_to(t_ref[h], (tq, t_ref.shape[2]))
            bias = pltpu.roll(table, 0, 1, stride=1, stride_axis=0)[:, n_prev:n_prev + n_keys]
            if banded:
                bias = jnp.where(ahead >= 0, jnp.where(ahead <= B_BAND_CHUNKS, bias, MASK_VALUE), MASK_VALUE)
            bias_ref[h] = bias

    scale = B_HD ** -0.5
    for h in range(heads):
        cols = slice(h * B_HD, (h + 1) * B_HD)
        q = q_ref[:, cols].astype(BF16)
        if head_rows:
            kp, vp = (r[pl.ds(h, n_prev, stride=heads), :] for r in (kp_ref, vp_ref))
        else:
            kp, vp = kp_ref[:, cols], vp_ref[:, cols]
        s0 = lax.dot_general(q, kp.astype(BF16), _NT, preferred_element_type=F32) * scale
        s1 = lax.dot_general(q, kc_ref[:, cols].astype(BF16), _NT, preferred_element_type=F32) * scale
        s0 += bias_ref[h, :, :n_prev]
        s1 += bias_ref[h, :, n_prev:]
        if banded:
            s0 = jnp.where(pl.program_id(2) == 0, MASK_VALUE, s0)
        m = jnp.maximum(jnp.max(s0, axis=-1, keepdims=True), jnp.max(s1, axis=-1, keepdims=True))
        p0 = jnp.exp(s0 - m)
        p1 = jnp.exp(s1 - m)
        l = jnp.sum(p0, axis=-1, keepdims=True) + jnp.sum(p1, axis=-1, keepdims=True)
        o = jnp.dot(p0.astype(BF16), vp.astype(BF16), preferred_element_type=F32)
        o += jnp.dot(p1.astype(BF16), vc_ref[:, cols].astype(BF16), preferred_element_type=F32)
        o_ref[:, cols] = (o / l).astype(o_ref.dtype)


def _rel_bias_table(rel_bias, n_prev, length):
    lo = 2 * n_prev - REL_CLIP
    hi = lo + 2 * REL_CLIP + 1
    assert 0 <= lo and hi <= length
    heads = rel_bias.shape[0]
    table = jnp.concatenate([jnp.broadcast_to(rel_bias[:, -1:], (heads, lo)),
                             rel_bias[:, ::-1],
                             jnp.broadcast_to(rel_bias[:, :1], (heads, length - hi))], axis=1)
    return table[:, None, :]


def _band_attention(q3, prev_k, prev_v, rel_bias, tok_width, heads_per_step, banded):
    bsz, t, _ = q3.shape
    tq = min(t, B_REACH)
    hw = heads_per_step * B_HD
    ng = tok_width // hw
    new = lambda idx: pl.BlockSpec((None, tq, hw), idx)
    if prev_k is None:
        n_prev = B_REACH
        before = lambda i: jnp.maximum(i - 1, 0)
        prev_k = prev_v = q3
        k_spec = pl.BlockSpec((None, n_prev, hw), lambda g, b, i: (b, before(i), ng + g))
        v_spec = pl.BlockSpec((None, n_prev, hw), lambda g, b, i: (b, before(i), 2 * ng + g))
    else:
        assert ng == 1 and prev_k.shape[2] == B_HD
        n_prev = prev_k.shape[1] // heads_per_step
        k_spec = v_spec = pl.BlockSpec((None,) + prev_k.shape[1:], lambda g, b, i: (b, 0, 0))
    assert t % tq == 0 and tok_width % hw == 0 and n_prev % LANES == 0 and tq <= n_prev
    n_keys = n_prev + tq
    length = pl.next_power_of_2(n_prev + n_keys)
    table = _rel_bias_table(rel_bias, n_prev, length)
    return pl.pallas_call(
        functools.partial(_band_kernel, heads=heads_per_step, banded=banded, head_rows=prev_k is not q3),
        grid=(ng, bsz, t // tq),
        in_specs=[new(lambda g, b, i: (b, i, g)),
                  k_spec,
                  new(lambda g, b, i: (b, i, ng + g)),
                  v_spec,
                  new(lambda g, b, i: (b, i, 2 * ng + g)),
                  pl.BlockSpec((heads_per_step, 1, length), lambda g, b, i: (g, 0, 0))],
        out_specs=new(lambda g, b, i: (b, i, g)),
        out_shape=jax.ShapeDtypeStruct((bsz, t, tok_width), BF16),
        scratch_shapes=[pltpu.VMEM((heads_per_step, tq, n_keys), F32)],
        compiler_params=_params("arbitrary", "arbitrary", "arbitrary"),
    )(q3, prev_k, q3, prev_v, q3, table)


def _top16_rows(vals, order, tie_break):
    rank = jnp.full(vals.shape, float(PEER_TOPK), F32)
    tops = []
    for r in range(PEER_TOPK):
        m = jnp.max(vals, axis=0, keepdims=True)
        hit = vals == m
        if tie_break:
            hit = order == jnp.min(jnp.where(hit, order, jnp.int32(2 ** 30)), axis=0, keepdims=True)
        rank = jnp.where(hit, float(r), rank)
        vals = jnp.where(hit, -jnp.inf, vals)
        tops.append(m)
    taken = jnp.sum(jnp.where(rank < float(PEER_TOPK), 1.0, 0.0), axis=0, keepdims=True)
    return jnp.concatenate(tops, axis=0), rank, taken


def _peer_select_kernel(q_ref, keys_ref, r2_ref, e2_ref, na_ref, f1_ref):
    most_taken = _peer_select_pass(q_ref, keys_ref, r2_ref, e2_ref, na_ref, f1_ref, tie_break=False)

    @pl.when(jnp.max(most_taken) > float(PEER_TOPK))
    def _():
        _peer_select_pass(q_ref, keys_ref, r2_ref, e2_ref, na_ref, f1_ref, tie_break=True)


def _peer_select_pass(q_ref, keys_ref, r2_ref, e2_ref, na_ref, f1_ref, tie_break):
    nk, k = PEER_NKEYS, PEER_TOPK
    tq = q_ref.shape[0]
    row = lax.broadcasted_iota(jnp.int32, (nk, tq), 0)
    jrow = lax.broadcasted_iota(jnp.int32, (k, tq), 0)
    n_full = k // 2
    crow = lax.broadcasted_iota(jnp.int32, (n_full * k + k - n_full, tq), 0)
    flat = jnp.where(crow < n_full * k, crow, (crow - n_full * k + n_full) * k)
    most_taken = jnp.zeros((1, tq), F32)
    for h in range(PEER_HEADS):
        tops, ranks, exps = [], [], []
        for p in range(2):
            c0 = (h * 2 + p) * nk
            qc = q_ref[:, c0:c0 + nk].astype(BF16)
            s = lax.dot_general(keys_ref[h, p], qc, _NT, preferred_element_type=F32)
            t, r, taken = _top16_rows(s, row, tie_break)
            most_taken = jnp.maximum(most_taken, taken)
            tops.append(t)
            ranks.append(r)
            exps.append(jnp.exp(s - t[0:1]))
        t1, t2 = tops
        e1r = jnp.exp(t1 - t1[0:1])
        e2r = jnp.exp(t2 - t2[0:1])
        cand, wgt = [], []
        for i in range(n_full):
            c = t1[i:i + 1] + t2
            cand.append(jnp.where(jrow < k // (i + 1), c, -jnp.inf))
            wgt.append(e1r[i:i + 1] * e2r)
        cand.append(t1[n_full:] + t2[0:1])
        wgt.append(e1r[n_full:] * e2r[0:1])
        cand = jnp.concatenate(cand, axis=0)
        wgt = jnp.concatenate(wgt, axis=0)
        _, crank, taken = _top16_rows(cand, flat, tie_break)
        most_taken = jnp.maximum(most_taken, taken)
        sel = jnp.where(crank < float(k), 1.0, 0.0)
        z = jnp.sum(sel * wgt, axis=0, keepdims=True)
        na = jnp.zeros((nk, tq), F32)
        for i in range(k):
            if i < n_full:
                n_i = jnp.sum(sel[i * k:(i + 1) * k], axis=0, keepdims=True)
            else:
                n_i = sel[n_full * k + i - n_full:n_full * k + i - n_full + 1]
            na = jnp.where(ranks[0] == float(i), n_i, na)
        r2_ref[h] = pltpu.bitcast(ranks[1].astype(BF16), jnp.uint32)
        e2_ref[h] = pltpu.bitcast(exps[1].astype(BF16), jnp.uint32)
        f1 = exps[0] / z
        ta = na_ref.shape[2]
        for g in range(nk // ta):
            na_ref[h, g] = na[g * ta:(g + 1) * ta]
            f1_ref[h, g] = f1[g * ta:(g + 1) * ta]
    return most_taken


def _peer_select(q, keys, ta, tq=128):
    n = q.shape[0]
    assert n % tq == 0 and keys.shape == (PEER_HEADS, 2, PEER_NKEYS, q.shape[1] // (2 * PEER_HEADS))
    assert keys.shape[3] == PEER_NKEYS
    out = jax.ShapeDtypeStruct((PEER_HEADS, PEER_NKEYS // 2, n), jnp.uint32)
    spec = pl.BlockSpec((PEER_HEADS, PEER_NKEYS // 2, tq), lambda i: (0, 0, i))
    out_g = jax.ShapeDtypeStruct((PEER_HEADS, PEER_NKEYS // ta, ta, n), F32)
    spec_g = pl.BlockSpec((PEER_HEADS, PEER_NKEYS // ta, ta, tq), lambda i: (0, 0, 0, i))
    return pl.pallas_call(
        _peer_select_kernel,
        grid=(n // tq,),
        in_specs=[pl.BlockSpec((tq, q.shape[1]), lambda i: (i, 0)),
                  pl.BlockSpec(keys.shape, lambda i: (0, 0, 0, 0))],
        out_specs=[spec, spec, spec_g, spec_g],
        out_shape=[out, out, out_g, out_g],
        compiler_params=_params("parallel"),
    )(q, keys)


def _peer_expert_kernel(x_ref, u_ref, v_ref, r2_ref, e2_ref, na_ref, f1_ref, res_hbm, g_ref, b_ref,
                        o_ref, act_ref, a_ref, res_sem):
    i, j = pl.program_id(0), pl.program_id(1)
    te, tm = act_ref.shape
    nk = PEER_NKEYS

    def residual_copy():
        rows = pl.ds(pl.multiple_of(i * tm, tm), tm)
        return pltpu.make_async_copy(res_hbm.at[rows, :], o_ref, res_sem)

    @pl.when(j == 0)
    def _():
        residual_copy().start()
        residual_copy().wait()
        o_ref[...] = ALPHA * o_ref[...]

    act_ref[...] = lax.dot_general(u_ref[...], x_ref[...], _NT, preferred_element_type=F32)
    zero = jnp.zeros((), BF16)

    def rows_bf16(row):
        bits = pltpu.bitcast(row.astype(BF16).astype(F32), jnp.uint32) & jnp.uint32(0xFFFF0000)
        return pltpu.bitcast(jnp.broadcast_to(bits | (bits >> 16), (nk // 2, LANES)), BF16)

    for a in range(te // nk):
        rows = slice(a * nk, (a + 1) * nk)
        for c in range(tm // LANES):
            lanes = slice(c * LANES, (c + 1) * LANES)
            gate = jnp.zeros((nk, LANES), BF16)
            for h in range(PEER_HEADS):
                na = rows_bf16(na_ref[h, a:a + 1, lanes])
                f1 = rows_bf16(f1_ref[h, a:a + 1, lanes])
                r2 = pltpu.bitcast(r2_ref[h, :, lanes], BF16)
                e2 = pltpu.bitcast(e2_ref[h, :, lanes], BF16)
                gate += jnp.where(r2 < na, e2, zero) * f1
            a_ref[rows, lanes] = gate * jax.nn.gelu(act_ref[rows, lanes]).astype(BF16)

    o_ref[...] += lax.dot_general(a_ref[...], v_ref[...], _TN, preferred_element_type=F32)

    @pl.when(j == pl.num_programs(1) - 1)
    def _():
        o_ref[...] = _layer_norm(o_ref[...], g_ref[...], b_ref[...])


def _peer_experts(x_bf, u, v, r2, e2, na, f1, residual, g, b, tm=512):
    n, d = x_bf.shape
    ta = na.shape[2]
    te = ta * PEER_NKEYS
    n_exp = u.shape[0]
    tm = min(tm, n)
    assert n % tm == 0 and n_exp == PEER_NKEYS * PEER_NKEYS
    gate_spec = pl.BlockSpec((PEER_HEADS, PEER_NKEYS // 2, tm), lambda i, j: (0, 0, i))
    row_spec = pl.BlockSpec((PEER_HEADS, None, ta, tm), lambda i, j: (0, j, 0, i))
    return pl.pallas_call(
        _peer_expert_kernel,
        grid=(n // tm, n_exp // te),
        in_specs=[pl.BlockSpec((tm, d), lambda i, j: (i, 0)),
                  pl.BlockSpec((te, d), lambda i, j: (j, 0)),
                  pl.BlockSpec((te, d), lambda i, j: (j, 0)),
                  gate_spec, gate_spec, row_spec, row_spec,
                  pl.BlockSpec(memory_space=pl.ANY),
                  pl.BlockSpec((1, d), lambda i, j: (0, 0)),
                  pl.BlockSpec((1, d), lambda i, j: (0, 0))],
        out_specs=pl.BlockSpec((tm, d), lambda i, j: (i, 0)),
        out_shape=jax.ShapeDtypeStruct((n, d), F32),
        scratch_shapes=[pltpu.VMEM((te, tm), F32), pltpu.VMEM((te, tm), BF16), pltpu.SemaphoreType.DMA(())],
        compiler_params=_params("arbitrary", "arbitrary"),
    )(x_bf, u, v, r2, e2, na, f1, residual, g.reshape(1, d), b.reshape(1, d))


def _residual_block(x, tok, h3, mk, k_col, mv, v_col, head_rows, mem_width,
                    w_out, ln1g, ln1b, ln2g, ln2b, wq, keys, u, v):
    rows = x.shape[0]
    mo = _mem_attn(h3, mem_width, mk, k_col, mv, v_col, head_rows).reshape(rows, mem_width)
    x1, x1_bf = _out_proj_ln(tok, mo, w_out, x, ln1g, ln1b)
    q = _matmul(x1_bf, wq)
    r2, e2, na, f1 = _peer_select(q, keys, PEER_TILE_KEYS)
    return _peer_experts(x1_bf, u, v, r2, e2, na, f1, x1, ln2g, ln2b)


def kernel(x_prompt, x_sample, cache_b_k, cache_b_v, cache_mem_k, cache_mem_v, mem_prompt, w_in_a, sgu_ln_g, sgu_ln_b, sgu_w, sgu_b, w_in_b, rel_bias, w_mem_kv, w_out, ln1_g, ln1_b, ln2_g, ln2_b, peer_wq, peer_keys, peer_u, peer_v):
    bp, s, d = x_prompt.shape
    bs, t, _ = x_sample.shape
    depth = w_out.shape[0]
    assert depth == DEPTH
    n_mem = mem_prompt.shape[1]
    mem_width = w_mem_kv.shape[2] // 2
    tok_width = w_out.shape[1] - mem_width
    mem_hd = mem_width // MEM_HEADS
    b_heads = tok_width // B_HD
    keep = min(B_REACH, s)

    xp = x_prompt.reshape(bp * s, d)
    xs = x_sample.reshape(bs * t, d)
    kb_p, vb_p, mk_p_all, mv_p_all, kb_s, vb_s, va_s = [], [], [], [], [], [], []
    for i in range(depth):
        j = i // 2
        kv_p = _matmul(mem_prompt.reshape(bp * n_mem, d), _layer_weight_bf16(w_mem_kv, i))
        kv_p = kv_p.reshape(bp, n_mem, 2 * mem_width)
        mk_p_all.append(kv_p[:, :, :mem_width].reshape(bp, n_mem, MEM_HEADS, mem_hd))
        mv_p_all.append(kv_p[:, :, mem_width:].reshape(bp, n_mem, MEM_HEADS, mem_hd))
        if i % 2 == 0:
            w_in = _layer_weight_bf16(w_in_a, j)
            h_p = _matmul(xp, w_in)
            h_s = _matmul(xs, w_in)
            tok_p, _ = _sgu(h_p, tok_width, s, sgu_ln_g[j], sgu_ln_b[j], sgu_w[j], sgu_b[j], want_v=False)
            tok_s, v_rows = _sgu(h_s, tok_width, t, sgu_ln_g[j], sgu_ln_b[j], sgu_w[j], sgu_b[j], want_v=True)
            va_s.append(v_rows.reshape(bs, t, tok_width))
            h_p = h_p.reshape(bp, s, -1)
            h_s = h_s.reshape(bs, t, -1)
        else:
            w_in = _layer_weight_bf16(w_in_b, j)
            h_p = _matmul(xp, w_in).reshape(bp, s, -1)
            h_s = _matmul(xs, w_in).reshape(bs, t, -1)
            tok_p = _band_attention(h_p, None, None, rel_bias[j], tok_width, 2, banded=True)
            tok_p = tok_p.reshape(bp * s, tok_width)
            ck = cache_b_k[j].reshape(bs, -1, B_HD)
            cv = cache_b_v[j].reshape(bs, -1, B_HD)
            tok_s = _band_attention(h_s, ck, cv, rel_bias[j], tok_width, b_heads, banded=False)
            tok_s = tok_s.reshape(bs * t, tok_width)
            kb_p.append(h_p[:, s - keep:, tok_width:2 * tok_width].reshape(bp, keep, b_heads, B_HD))
            vb_p.append(h_p[:, s - keep:, 2 * tok_width:3 * tok_width].reshape(bp, keep, b_heads, B_HD))
            kb_s.append(h_s[:, :, tok_width:2 * tok_width].reshape(bs, t, b_heads, B_HD))
            vb_s.append(h_s[:, :, 2 * tok_width:3 * tok_width].reshape(bs, t, b_heads, B_HD))
        shared = (_layer_weight_bf16(w_out, i), ln1_g[i], ln1_b[i], ln2_g[i], ln2_b[i],
                  _layer_weight_bf16(peer_wq, i), peer_keys[i].astype(BF16),
                  _layer_weight_bf16(peer_u, i), _layer_weight_bf16(peer_v, i))
        xp = _residual_block(xp, tok_p, h_p, kv_p, 0, kv_p, 1, False, mem_width, *shared)
        mk_s = cache_mem_k[i].reshape(bs, -1, LANES)
        mv_s = cache_mem_v[i].reshape(bs, -1, LANES)
        xs = _residual_block(xs, tok_s, h_s, mk_s, 0, mv_s, 0, True, mem_width, *shared)
    return (xp.reshape(bp, s, d), xs.reshape(bs, t, d), jnp.stack(kb_p), jnp.stack(vb_p),
            jnp.stack(mk_p_all), jnp.stack(mv_p_all), jnp.stack(kb_s), jnp.stack(vb_s), jnp.stack(va_s))
```

```python
import functools

import jax
import jax.numpy as jnp
from jax import lax
from jax.experimental import pallas as pl
from jax.experimental.pallas import tpu as pltpu

F32 = jnp.float32
BF16 = jnp.bfloat16

DEPTH = 2
CHUNK = 64
MEM_HEADS = 4
SGU_CHUNK = 128
SGU_GROUPS = 8
B_HD = 128
B_BAND_CHUNKS = 8
B_REACH = B_BAND_CHUNKS * CHUNK
REL_CLIP = 128
PEER_HEADS = 8
PEER_NKEYS = 128
PEER_TOPK = 16
PEER_TILE_KEYS = 4
ALPHA = (2 * DEPTH) ** 0.25
LN_EPS = 1e-5
MASK_VALUE = -1e30

VMEM_LIMIT_BYTES = 56 * 1024 * 1024
LANES = 128

_NT = (((1,), (1,)), ((), ()))
_TN = (((0,), (0,)), ((), ()))


def _params(*semantics):
    return pltpu.CompilerParams(dimension_semantics=semantics, vmem_limit_bytes=VMEM_LIMIT_BYTES)


def _layer_norm(x, g, b):
    mu = jnp.mean(x, axis=-1, keepdims=True)
    xc = x - mu
    var = jnp.mean(xc * xc, axis=-1, keepdims=True)
    return xc * lax.rsqrt(var + LN_EPS) * g + b


CAST_BLOCK_BYTES = 8 * 1024 * 1024


def _cast_kernel(w_ref, o_ref):
    o_ref[...] = w_ref[...].astype(o_ref.dtype)


def _layer_weight_bf16(w, layer):
    _, r, c = w.shape
    rows = r
    while rows * c * w.dtype.itemsize > CAST_BLOCK_BYTES and rows % 16 == 0:
        rows //= 2
    return pl.pallas_call(
        _cast_kernel,
        grid=(r // rows,),
        in_specs=[pl.BlockSpec((None, rows, c), lambda i: (layer, i, 0))],
        out_specs=pl.BlockSpec((rows, c), lambda i: (i, 0)),
        out_shape=jax.ShapeDtypeStruct((r, c), BF16),
        compiler_params=_params("parallel"),
    )(w)


def _mm_kernel(a_ref, w_ref, o_ref, abf_ref):
    @pl.when(pl.program_id(1) == 0)
    def _():
        abf_ref[...] = a_ref[...].astype(BF16)

    o_ref[...] = jnp.dot(abf_ref[...], w_ref[...], preferred_element_type=F32).astype(o_ref.dtype)


def _mm_bf16_kernel(a_ref, w_ref, o_ref):
    o_ref[...] = jnp.dot(a_ref[...], w_ref[...], preferred_element_type=F32).astype(o_ref.dtype)


def _matmul(a, w, out_dtype=F32):
    m, k = a.shape
    n = w.shape[1]
    cast = a.dtype != BF16
    tm = min(512 if cast else 1024, m)
    tn = next(c for c in (1024, 512, 256, LANES) if n % c == 0)
    assert m % tm == 0
    return pl.pallas_call(
        _mm_kernel if cast else _mm_bf16_kernel,
        grid=(m // tm, n // tn),
        in_specs=[pl.BlockSpec((tm, k), lambda i, j: (i, 0)),
                  pl.BlockSpec((k, tn), lambda i, j: (0, j))],
        out_specs=pl.BlockSpec((tm, tn), lambda i, j: (i, j)),
        out_shape=jax.ShapeDtypeStruct((m, n), out_dtype),
        scratch_shapes=[pltpu.VMEM((tm, k), BF16)] if cast else [],
        compiler_params=_params("parallel", "arbitrary"),
    )(a, w)


def _out_ln_kernel(tok_ref, mo_ref, w1_ref, w2_ref, x_ref, g_ref, b_ref, o_ref, obf_ref, *, tn, nj):
    j = pl.program_id(1)
    acc = jnp.dot(tok_ref[...], w1_ref[...], preferred_element_type=F32)
    acc += jnp.dot(mo_ref[...], w2_ref[...], preferred_element_type=F32)
    col = pl.multiple_of(j * tn, tn)
    o_ref[:, pl.ds(col, tn)] = ALPHA * x_ref[...] + acc

    @pl.when(j == nj - 1)
    def _():
        y = _layer_norm(o_ref[...], g_ref[...], b_ref[...])
        o_ref[...] = y
        obf_ref[...] = y.astype(BF16)


def _out_proj_ln(tok, mo, w_out, x, g, b, tm=512, tn=512):
    m, d = x.shape
    kt, km = tok.shape[1], mo.shape[1]
    assert kt % km == 0 and w_out.shape == (kt + km, d)
    tm, tn = min(tm, m), min(tn, d)
    assert m % tm == 0 and d % tn == 0
    nj = d // tn
    return pl.pallas_call(
        functools.partial(_out_ln_kernel, tn=tn, nj=nj),
        grid=(m // tm, nj),
        in_specs=[pl.BlockSpec((tm, kt), lambda i, j: (i, 0)),
                  pl.BlockSpec((tm, km), lambda i, j: (i, 0)),
                  pl.BlockSpec((kt, tn), lambda i, j: (0, j)),
                  pl.BlockSpec((km, tn), lambda i, j: (kt // km, j)),
                  pl.BlockSpec((tm, tn), lambda i, j: (i, j)),
                  pl.BlockSpec((1, d), lambda i, j: (0, 0)),
                  pl.BlockSpec((1, d), lambda i, j: (0, 0))],
        out_specs=[pl.BlockSpec((tm, d), lambda i, j: (i, 0)),
                   pl.BlockSpec((tm, d), lambda i, j: (i, 0))],
        out_shape=[jax.ShapeDtypeStruct((m, d), F32), jax.ShapeDtypeStruct((m, d), BF16)],
        compiler_params=_params("parallel", "arbitrary"),
    )(tok, mo, w_out, w_out, x, g.reshape(1, d), b.reshape(1, d))


def _sgu_kernel(u_ref, v_ref, g_ref, b_ref, w_ref, bias_ref, tok_ref, *vn_refs, groups):
    n = u_ref.shape[0]
    gw = u_ref.shape[1] // groups
    vn = _layer_norm(jax.nn.gelu(v_ref[...]), g_ref[...], b_ref[...])
    if vn_refs:
        vn_refs[0][...] = vn
    causal = lax.broadcasted_iota(jnp.int32, (n, n), 1) <= lax.broadcasted_iota(jnp.int32, (n, n), 0)
    for g in range(groups):
        cols = slice(g * gw, (g + 1) * gw)
        w = jnp.where(causal, w_ref[g], 0.0).astype(BF16)
        s = jnp.dot(w, vn[:, cols].astype(BF16), preferred_element_type=F32) + bias_ref[:, cols]
        tok_ref[:, cols] = (jax.nn.gelu(u_ref[:, cols]) * s).astype(tok_ref.dtype)


def _sgu(h, tok_width, t, ln_g, ln_b, w_s, b_s, want_v):
    rows = h.shape[0]
    n = min(t, SGU_CHUNK)
    assert t % n == 0 and tok_width % SGU_GROUPS == 0
    gw = tok_width // SGU_GROUPS
    w = w_s[:, :n, :n]
    bias = jnp.repeat(b_s[:, :n].T, gw, axis=1)
    out_shape = [jax.ShapeDtypeStruct((rows, tok_width), BF16)]
    out_specs = [pl.BlockSpec((n, tok_width), lambda c: (c, 0))]
    if want_v:
        out_shape.append(jax.ShapeDtypeStruct((rows, tok_width), F32))
        out_specs.append(pl.BlockSpec((n, tok_width), lambda c: (c, 0)))
    vec = pl.BlockSpec((1, tok_width), lambda c: (0, 0))
    res = pl.pallas_call(
        functools.partial(_sgu_kernel, groups=SGU_GROUPS),
        grid=(rows // n,),
        in_specs=[pl.BlockSpec((n, tok_width), lambda c: (c, 0)),
                  pl.BlockSpec((n, tok_width), lambda c: (c, 1)),
                  vec, vec,
                  pl.BlockSpec((SGU_GROUPS, n, n), lambda c: (0, 0, 0)),
                  pl.BlockSpec((n, tok_width), lambda c: (0, 0))],
        out_specs=out_specs,
        out_shape=out_shape,
        compiler_params=_params("parallel"),
    )(h, h, ln_g.reshape(1, -1), ln_b.reshape(1, -1), w, bias)
    return res if want_v else (res[0], None)


def _mem_attn_kernel(q_ref, mk_ref, mv_ref, o_ref, *, head_rows):
    hd = q_ref.shape[1] // MEM_HEADS
    pieces = hd // LANES
    for h in range(MEM_HEADS):
        def piece(ref, c):
            if head_rows:
                period = MEM_HEADS * pieces
                return ref[pl.ds(h * pieces + c, ref.shape[0] // period, stride=period), :].astype(BF16)
            return ref[:, h * hd + c * LANES:h * hd + (c + 1) * LANES].astype(BF16)

        s = 0.0
        for c in range(pieces):
            q = q_ref[:, h * hd + c * LANES:h * hd + (c + 1) * LANES].astype(BF16)
            s += lax.dot_general(q, piece(mk_ref, c), _NT, preferred_element_type=F32)
        s *= hd ** -0.5
        p = jnp.exp(s - jnp.max(s, axis=-1, keepdims=True))
        l = jnp.sum(p, axis=-1, keepdims=True)
        p = p.astype(BF16)
        for c in range(pieces):
            o = jnp.dot(p, piece(mv_ref, c), preferred_element_type=F32)
            o_ref[:, h * hd + c * LANES:h * hd + (c + 1) * LANES] = (o / l).astype(o_ref.dtype)


def _mem_attn(h3, mem_width, mk, k_col, mv, v_col, head_rows, tq=512):
    bsz, t, w = h3.shape
    assert w % mem_width == 0
    tq = min(tq, t)
    assert t % tq == 0
    q_col = w // mem_width - 1
    kv_block = mk.shape[1:] if head_rows else (mk.shape[1], mem_width)
    return pl.pallas_call(
        functools.partial(_mem_attn_kernel, head_rows=head_rows),
        grid=(bsz, t // tq),
        in_specs=[pl.BlockSpec((None, tq, mem_width), lambda b, i: (b, i, q_col)),
                  pl.BlockSpec((None,) + kv_block, lambda b, i: (b, 0, k_col)),
                  pl.BlockSpec((None,) + kv_block, lambda b, i: (b, 0, v_col))],
        out_specs=pl.BlockSpec((None, tq, mem_width), lambda b, i: (b, i, 0)),
        out_shape=jax.ShapeDtypeStruct((bsz, t, mem_width), BF16),
        compiler_params=_params("parallel", "parallel"),
    )(h3, mk, mv)


def _band_kernel(q_ref, kp_ref, kc_ref, vp_ref, vc_ref, t_ref, o_ref, bias_ref, *, heads, banded, head_rows):
    tq = q_ref.shape[0]
    n_keys = bias_ref.shape[2]
    n_prev = n_keys - tq

    @pl.when((pl.program_id(1) == 0) & (pl.program_id(2) == 0))
    def _():
        row = lax.broadcasted_iota(jnp.int32, (tq, n_keys), 0)
        col = lax.broadcasted_iota(jnp.int32, (tq, n_keys), 1)
        ahead = col // CHUNK - row // CHUNK
        for h in range(heads):
            table = jnp.broadcast_to(t_ref[h], (tq, t_ref.shape[2]))
            bias = pltpu.roll(table, 0, 1, stride=1, stride_axis=0)[:, n_prev:n_prev + n_keys]
            if banded:
                bias = jnp.where(ahead >= 0, jnp.where(ahead <= B_BAND_CHUNKS, bias, MASK_VALUE), MASK_VALUE)
            bias_ref[h] = bias

    scale = B_HD ** -0.5
    for h in range(heads):
        cols = slice(h * B_HD, (h + 1) * B_HD)
        q = q_ref[:, cols].astype(BF16)
        if head_rows:
            kp, vp = (r[pl.ds(h, n_prev, stride=heads), :] for r in (kp_ref, vp_ref))
        else:
            kp, vp = kp_ref[:, cols], vp_ref[:, cols]
        s0 = lax.dot_general(q, kp.astype(BF16), _NT, preferred_element_type=F32) * scale
        s1 = lax.dot_general(q, kc_ref[:, cols].astype(BF16), _NT, preferred_element_type=F32) * scale
        s0 += bias_ref[h, :, :n_prev]
        s1 += bias_ref[h, :, n_prev:]
        if banded:
            s0 = jnp.where(pl.program_id(2) == 0, MASK_VALUE, s0)
        m = jnp.maximum(jnp.max(s0, axis=-1, keepdims=True), jnp.max(s1, axis=-1, keepdims=True))
        p0 = jnp.exp(s0 - m)
        p1 = jnp.exp(s1 - m)
        l = jnp.sum(p0, axis=-1, keepdims=True) + jnp.sum(p1, axis=-1, keepdims=True)
        o = jnp.dot(p0.astype(BF16), vp.astype(BF16), preferred_element_type=F32)
        o += jnp.dot(p1.astype(BF16), vc_ref[:, cols].astype(BF16), preferred_element_type=F32)
        o_ref[:, cols] = (o / l).astype(o_ref.dtype)


def _rel_bias_table(rel_bias, n_prev, length):
    lo = 2 * n_prev - REL_CLIP
    hi = lo + 2 * REL_CLIP + 1
    assert 0 <= lo and hi <= length
    heads = rel_bias.shape[0]
    table = jnp.concatenate([jnp.broadcast_to(rel_bias[:, -1:], (heads, lo)),
                             rel_bias[:, ::-1],
                             jnp.broadcast_to(rel_bias[:, :1], (heads, length - hi))], axis=1)
    return table[:, None, :]


def _band_attention(q3, prev_k, prev_v, rel_bias, tok_width, heads_per_step, banded):
    bsz, t, _ = q3.shape
    tq = min(t, B_REACH)
    hw = heads_per_step * B_HD
    ng = tok_width // hw
    new = lambda idx: pl.BlockSpec((None, tq, hw), idx)
    if prev_k is None:
        n_prev = B_REACH
        before = lambda i: jnp.maximum(i - 1, 0)
        prev_k = prev_v = q3
        k_spec = pl.BlockSpec((None, n_prev, hw), lambda g, b, i: (b, before(i), ng + g))
        v_spec = pl.BlockSpec((None, n_prev, hw), lambda g, b, i: (b, before(i), 2 * ng + g))
    else:
        assert ng == 1 and prev_k.shape[2] == B_HD
        n_prev = prev_k.shape[1] // heads_per_step
        k_spec = v_spec = pl.BlockSpec((None,) + prev_k.shape[1:], lambda g, b, i: (b, 0, 0))
    assert t % tq == 0 and tok_width % hw == 0 and n_prev % LANES == 0 and tq <= n_prev
    n_keys = n_prev + tq
    length = pl.next_power_of_2(n_prev + n_keys)
    table = _rel_bias_table(rel_bias, n_prev, length)
    return pl.pallas_call(
        functools.partial(_band_kernel, heads=heads_per_step, banded=banded, head_rows=prev_k is not q3),
        grid=(ng, bsz, t // tq),
        in_specs=[new(lambda g, b, i: (b, i, g)),
                  k_spec,
                  new(lambda g, b, i: (b, i, ng + g)),
                  v_spec,
                  new(lambda g, b, i: (b, i, 2 * ng + g)),
                  pl.BlockSpec((heads_per_step, 1, length), lambda g, b, i: (g, 0, 0))],
        out_specs=new(lambda g, b, i: (b, i, g)),
        out_shape=jax.ShapeDtypeStruct((bsz, t, tok_width), BF16),
        scratch_shapes=[pltpu.VMEM((heads_per_step, tq, n_keys), F32)],
        compiler_params=_params("arbitrary", "arbitrary", "arbitrary"),
    )(q3, prev_k, q3, prev_v, q3, table)


def _top16_rows(vals, order, tie_break):
    rank = jnp.full(vals.shape, float(PEER_TOPK), F32)
    tops = []
    for r in range(PEER_TOPK):
        m = jnp.max(vals, axis=0, keepdims=True)
        hit = vals == m
        if tie_break:
            hit = order == jnp.min(jnp.where(hit, order, jnp.int32(2 ** 30)), axis=0, keepdims=True)
        rank = jnp.where(hit, float(r), rank)
        vals = jnp.where(hit, -jnp.inf, vals)
        tops.append(m)
    taken = jnp.sum(jnp.where(rank < float(PEER_TOPK), 1.0, 0.0), axis=0, keepdims=True)
    return jnp.concatenate(tops, axis=0), rank, taken


def _peer_select_kernel(q_ref, keys_ref, r2_ref, e2_ref, na_ref, f1_ref):
    most_taken = _peer_select_pass(q_ref, keys_ref, r2_ref, e2_ref, na_ref, f1_ref, tie_break=False)

    @pl.when(jnp.max(most_taken) > float(PEER_TOPK))
    def _():
        _peer_select_pass(q_ref, keys_ref, r2_ref, e2_ref, na_ref, f1_ref, tie_break=True)


def _peer_select_pass(q_ref, keys_ref, r2_ref, e2_ref, na_ref, f1_ref, tie_break):
    nk, k = PEER_NKEYS, PEER_TOPK
    tq = q_ref.shape[0]
    row = lax.broadcasted_iota(jnp.int32, (nk, tq), 0)
    jrow = lax.broadcasted_iota(jnp.int32, (k, tq), 0)
    n_full = k // 2
    crow = lax.broadcasted_iota(jnp.int32, (n_full * k + k - n_full, tq), 0)
    flat = jnp.where(crow < n_full * k, crow, (crow - n_full * k + n_full) * k)
    most_taken = jnp.zeros((1, tq), F32)
    for h in range(PEER_HEADS):
        tops, ranks, exps = [], [], []
        for p in range(2):
            c0 = (h * 2 + p) * nk
            qc = q_ref[:, c0:c0 + nk].astype(BF16)
            s = lax.dot_general(keys_ref[h, p], qc, _NT, preferred_element_type=F32)
            t, r, taken = _top16_rows(s, row, tie_break)
            most_taken = jnp.maximum(most_taken, taken)
            tops.append(t)
            ranks.append(r)
            exps.append(jnp.exp(s - t[0:1]))
        t1, t2 = tops
        e1r = jnp.exp(t1 - t1[0:1])
        e2r = jnp.exp(t2 - t2[0:1])
        cand, wgt = [], []
        for i in range(n_full):
            c = t1[i:i + 1] + t2
            cand.append(jnp.where(jrow < k // (i + 1), c, -jnp.inf))
            wgt.append(e1r[i:i + 1] * e2r)
        cand.append(t1[n_full:] + t2[0:1])
        wgt.append(e1r[n_full:] * e2r[0:1])
        cand = jnp.concatenate(cand, axis=0)
        wgt = jnp.concatenate(wgt, axis=0)
        _, crank, taken = _top16_rows(cand, flat, tie_break)
        most_taken = jnp.maximum(most_taken, taken)
        sel = jnp.where(crank < float(k), 1.0, 0.0)
        z = jnp.sum(sel * wgt, axis=0, keepdims=True)
        na = jnp.zeros((nk, tq), F32)
        for i in range(k):
            if i < n_full:
                n_i = jnp.sum(sel[i * k:(i + 1) * k], axis=0, keepdims=True)
            else:
                n_i = sel[n_full * k + i - n_full:n_full * k + i - n_full + 1]
            na = jnp.where(ranks[0] == float(i), n_i, na)
        r2_ref[h] = pltpu.bitcast(ranks[1].astype(BF16), jnp.uint32)
        e2_ref[h] = pltpu.bitcast(exps[1].astype(BF16), jnp.uint32)
        f1 = exps[0] / z
        ta = na_ref.shape[2]
        for g in range(nk // ta):
            na_ref[h, g] = na[g * ta:(g + 1) * ta]
            f1_ref[h, g] = f1[g * ta:(g + 1) * ta]
    return most_taken


def _peer_select(q, keys, ta, tq=128):
    n = q.shape[0]
    assert n % tq == 0 and keys.shape == (PEER_HEADS, 2, PEER_NKEYS, q.shape[1] // (2 * PEER_HEADS))
    assert keys.shape[3] == PEER_NKEYS
    out = jax.ShapeDtypeStruct((PEER_HEADS, PEER_NKEYS // 2, n), jnp.uint32)
    spec = pl.BlockSpec((PEER_HEADS, PEER_NKEYS // 2, tq), lambda i: (0, 0, i))
    out_g = jax.ShapeDtypeStruct((PEER_HEADS, PEER_NKEYS // ta, ta, n), F32)
    spec_g = pl.BlockSpec((PEER_HEADS, PEER_NKEYS // ta, ta, tq), lambda i: (0, 0, 0, i))
    return pl.pallas_call(
        _peer_select_kernel,
        grid=(n // tq,),
        in_specs=[pl.BlockSpec((tq, q.shape[1]), lambda i: (i, 0)),
                  pl.BlockSpec(keys.shape, lambda i: (0, 0, 0, 0))],
        out_specs=[spec, spec, spec_g, spec_g],
        out_shape=[out, out, out_g, out_g],
        compiler_params=_params("parallel"),
    )(q, keys)


def _peer_expert_kernel(x_ref, u_ref, v_ref, r2_ref, e2_ref, na_ref, f1_ref, res_hbm, g_ref, b_ref,
                        o_ref, act_ref, a_ref, res_sem):
    i, j = pl.program_id(0), pl.program_id(1)
    te, tm = act_ref.shape
    nk = PEER_NKEYS

    def residual_copy():
        rows = pl.ds(pl.multiple_of(i * tm, tm), tm)
        return pltpu.make_async_copy(res_hbm.at[rows, :], o_ref, res_sem)

    @pl.when(j == 0)
    def _():
        residual_copy().start()
        residual_copy().wait()
        o_ref[...] = ALPHA * o_ref[...]

    act_ref[...] = lax.dot_general(u_ref[...], x_ref[...], _NT, preferred_element_type=F32)
    zero = jnp.zeros((), BF16)

    def rows_bf16(row):
        bits = pltpu.bitcast(row.astype(BF16).astype(F32), jnp.uint32) & jnp.uint32(0xFFFF0000)
        return pltpu.bitcast(jnp.broadcast_to(bits | (bits >> 16), (nk // 2, LANES)), BF16)

    for a in range(te // nk):
        rows = slice(a * nk, (a + 1) * nk)
        for c in range(tm // LANES):
            lanes = slice(c * LANES, (c + 1) * LANES)
            gate = jnp.zeros((nk, LANES), BF16)
            for h in range(PEER_HEADS):
                na = rows_bf16(na_ref[h, a:a + 1, lanes])
                f1 = rows_bf16(f1_ref[h, a:a + 1, lanes])
                r2 = pltpu.bitcast(r2_ref[h, :, lanes], BF16)
                e2 = pltpu.bitcast(e2_ref[h, :, lanes], BF16)
                gate += jnp.where(r2 < na, e2, zero) * f1
            a_ref[rows, lanes] = gate * jax.nn.gelu(act_ref[rows, lanes]).astype(BF16)

    o_ref[...] += lax.dot_general(a_ref[...], v_ref[...], _TN, preferred_element_type=F32)

    @pl.when(j == pl.num_programs(1) - 1)
    def _():
        o_ref[...] = _layer_norm(o_ref[...], g_ref[...], b_ref[...])


def _peer_experts(x_bf, u, v, r2, e2, na, f1, residual, g, b, tm=512):
    n, d = x_bf.shape
    ta = na.shape[2]
    te = ta * PEER_NKEYS
    n_exp = u.shape[0]
    tm = min(tm, n)
    assert n % tm == 0 and n_exp == PEER_NKEYS * PEER_NKEYS
    gate_spec = pl.BlockSpec((PEER_HEADS, PEER_NKEYS // 2, tm), lambda i, j: (0, 0, i))
    row_spec = pl.BlockSpec((PEER_HEADS, None, ta, tm), lambda i, j: (0, j, 0, i))
    return pl.pallas_call(
        _peer_expert_kernel,
        grid=(n // tm, n_exp // te),
        in_specs=[pl.BlockSpec((tm, d), lambda i, j: (i, 0)),
                  pl.BlockSpec((te, d), lambda i, j: (j, 0)),
                  pl.BlockSpec((te, d), lambda i, j: (j, 0)),
                  gate_spec, gate_spec, row_spec, row_spec,
                  pl.BlockSpec(memory_space=pl.ANY),
                  pl.BlockSpec((1, d), lambda i, j: (0, 0)),
                  pl.BlockSpec((1, d), lambda i, j: (0, 0))],
        out_specs=pl.BlockSpec((tm, d), lambda i, j: (i, 0)),
        out_shape=jax.ShapeDtypeStruct((n, d), F32),
        scratch_shapes=[pltpu.VMEM((te, tm), F32), pltpu.VMEM((te, tm), BF16), pltpu.SemaphoreType.DMA(())],
        compiler_params=_params("arbitrary", "arbitrary"),
    )(x_bf, u, v, r2, e2, na, f1, residual, g.reshape(1, d), b.reshape(1, d))


def _residual_block(x, tok, h3, mk, k_col, mv, v_col, head_rows, mem_width,
                    w_out, ln1g, ln1b, ln2g, ln2b, wq, keys, u, v):
    rows = x.shape[0]
    mo = _mem_attn(h3, mem_width, mk, k_col, mv, v_col, head_rows).reshape(rows, mem_width)
    x1, x1_bf = _out_proj_ln(tok, mo, w_out, x, ln1g, ln1b)
    q = _matmul(x1_bf, wq)
    r2, e2, na, f1 = _peer_select(q, keys, PEER_TILE_KEYS)
    return _peer_experts(x1_bf, u, v, r2, e2, na, f1, x1, ln2g, ln2b)


def kernel(x_prompt, x_sample, cache_b_k, cache_b_v, cache_mem_k, cache_mem_v, mem_prompt, w_in_a, sgu_ln_g, sgu_ln_b, sgu_w, sgu_b, w_in_b, rel_bias, w_mem_kv, w_out, ln1_g, ln1_b, ln2_g, ln2_b, peer_wq, peer_keys, peer_u, peer_v):
    bp, s, d = x_prompt.shape
    bs, t, _ = x_sample.shape
    depth = w_out.shape[0]
    assert depth == DEPTH
    n_mem = mem_prompt.shape[1]
    mem_width = w_mem_kv.shape[2] // 2
    tok_width = w_out.shape[1] - mem_width
    mem_hd = mem_width // MEM_HEADS
    b_heads = tok_width // B_HD
    keep = min(B_REACH, s)

    xp = x_prompt.reshape(bp * s, d)
    xs = x_sample.reshape(bs * t, d)
    kb_p, vb_p, mk_p_all, mv_p_all, kb_s, vb_s, va_s = [], [], [], [], [], [], []
    for i in range(depth):
        j = i // 2
        kv_p = _matmul(mem_prompt.reshape(bp * n_mem, d), _layer_weight_bf16(w_mem_kv, i))
        kv_p = kv_p.reshape(bp, n_mem, 2 * mem_width)
        mk_p_all.append(kv_p[:, :, :mem_width].reshape(bp, n_mem, MEM_HEADS, mem_hd))
        mv_p_all.append(kv_p[:, :, mem_width:].reshape(bp, n_mem, MEM_HEADS, mem_hd))
        if i % 2 == 0:
            w_in = _layer_weight_bf16(w_in_a, j)
            h_p = _matmul(xp, w_in)
            h_s = _matmul(xs, w_in)
            tok_p, _ = _sgu(h_p, tok_width, s, sgu_ln_g[j], sgu_ln_b[j], sgu_w[j], sgu_b[j], want_v=False)
            tok_s, v_rows = _sgu(h_s, tok_width, t, sgu_ln_g[j], sgu_ln_b[j], sgu_w[j], sgu_b[j], want_v=True)
            va_s.append(v_rows.reshape(bs, t, tok_width))
            h_p = h_p.reshape(bp, s, -1)
            h_s = h_s.reshape(bs, t, -1)
        else:
            w_in = _layer_weight_bf16(w_in_b, j)
            h_p = _matmul(xp, w_in).reshape(bp, s, -1)
            h_s = _matmul(xs, w_in).reshape(bs, t, -1)
            tok_p = _band_attention(h_p, None, None, rel_bias[j], tok_width, 4, banded=True)
            tok_p = tok_p.reshape(bp * s, tok_width)
            ck = cache_b_k[j].reshape(bs, -1, B_HD)
            cv = cache_b_v[j].reshape(bs, -1, B_HD)
            tok_s = _band_attention(h_s, ck, cv, rel_bias[j], tok_width, b_heads, banded=False)
            tok_s = tok_s.reshape(bs * t, tok_width)
            kb_p.append(h_p[:, s - keep:, tok_width:2 * tok_width].reshape(bp, keep, b_heads, B_HD))
            vb_p.append(h_p[:, s - keep:, 2 * tok_width:3 * tok_width].reshape(bp, keep, b_heads, B_HD))
            kb_s.append(h_s[:, :, tok_width:2 * tok_width].reshape(bs, t, b_heads, B_HD))
            vb_s.append(h_s[:, :, 2 * tok_width:3 * tok_width].reshape(bs, t, b_heads, B_HD))
        shared = (_layer_weight_bf16(w_out, i), ln1_g[i], ln1_b[i], ln2_g[i], ln2_b[i],
                  _layer_weight_bf16(peer_wq, i), peer_keys[i].astype(BF16),
                  _layer_weight_bf16(peer_u, i), _layer_weight_bf16(peer_v, i))
        xp = _residual_block(xp, tok_p, h_p, kv_p, 0, kv_p, 1, False, mem_width, *shared)
        mk_s = cache_mem_k[i].reshape(bs, -1, LANES)
        mv_s = cache_mem_v[i].reshape(bs, -1, LANES)
        xs = _residual_block(xs, tok_s, h_s, mk_s, 0, mv_s, 0, True, mem_width, *shared)
    return (xp.reshape(bp, s, d), xs.reshape(bs, t, d), jnp.stack(kb_p), jnp.stack(vb_p),
            jnp.stack(mk_p_all), jnp.stack(mv_p_all), jnp.stack(kb_s), jnp.stack(vb_s), jnp.stack(va_s))
```
